```python
import jax
import jax.numpy as jnp
from jax import lax
import numpy as np

D_MODEL = 1024
BATCH = 4
SEQ = 4096
DEPTH = 4
DEC_BATCH = 128
DEC_SEQ = 1
PAST_LEN = 8192
PAGE_SIZE = 128

MLA_HEADS = 8
MLA_NOPE_DIM = 64
MLA_ROPE_DIM = 32
MLA_V_DIM = 64
MLA_Q_RANK = 384
MLA_KV_RANK = 256
MLA_SCALE = (MLA_NOPE_DIM + MLA_ROPE_DIM) ** -0.5
ROPE_THETA = 10000.0
CONV_DIM = 512
CONV_WIDTH = 3
SB_HEADS = 8
SB_KV_HEADS = 2
SB_HEAD_DIM = 64
SB_REP = SB_HEADS // SB_KV_HEADS
SB_SCALE = SB_HEAD_DIM ** -0.5
BRANCH_DIM = 512
N_BRANCHES = 3
N_EXPERTS = 32
TOP_K = 4
D_FF = D_MODEL
SWIGLU_LIMIT = 7.0
SWIGLU_ALPHA = 1.702
MOE_BLOCK = 128
Q_BLOCK = 128
LN_EPS = 1e-5
RMS_EPS = 1e-6
DEEPNORM_ALPHA = (2 * DEPTH) ** 0.25
DEEPNORM_BETA = (8 * DEPTH) ** -0.25
IN_SPLITS = (MLA_Q_RANK, MLA_KV_RANK, MLA_ROPE_DIM, CONV_DIM, CONV_DIM, CONV_DIM,
             SB_HEADS * SB_HEAD_DIM, SB_KV_HEADS * SB_HEAD_DIM, SB_KV_HEADS * SB_HEAD_DIM,
             N_BRANCHES * D_MODEL)
IN_COLS = sum(IN_SPLITS)

kernel_name = "hybrid_mla_conv_stickbreak_moe_step"


def layer_norm(x, g, b):
    xf = x.astype(jnp.float32)
    mu = jnp.mean(xf, axis=-1, keepdims=True)
    var = jnp.mean(jnp.square(xf - mu), axis=-1, keepdims=True)
    return ((xf - mu) * lax.rsqrt(var + LN_EPS) * g.astype(jnp.float32) + b.astype(jnp.float32)).astype(x.dtype)


def rms_norm(x, g):
    xf = x.astype(jnp.float32)
    return (xf * lax.rsqrt(jnp.mean(xf * xf, axis=-1, keepdims=True) + RMS_EPS) * g.astype(jnp.float32)).astype(x.dtype)


def rope(x, pos):
    half = x.shape[-1] // 2
    freq = ROPE_THETA ** (-jnp.arange(half, dtype=jnp.float32) / half)
    ang = pos.astype(jnp.float32)[:, None] * freq
    ang = ang.reshape(ang.shape[0], *([1] * (x.ndim - 3)), half)
    cos, sin = jnp.cos(ang), jnp.sin(ang)
    x1 = x[..., :half].astype(jnp.float32)
    x2 = x[..., half:].astype(jnp.float32)
    return jnp.concatenate([x1 * cos - x2 * sin, x1 * sin + x2 * cos], axis=-1).astype(x.dtype)


def split_cols(h):
    parts, start = [], 0
    for width in IN_SPLITS:
        parts.append(h[..., start:start + width])
        start += width
    return parts


def gather_pages(pool, layer, page_table):
    g = pool[layer, page_table]
    return g.reshape(g.shape[0], g.shape[1] * g.shape[2], *g.shape[3:])


def sweep_query_blocks(attend, qs, q_pos):
    n_q = q_pos.shape[0]
    qb = Q_BLOCK if n_q % Q_BLOCK == 0 else n_q
    nb = n_q // qb

    def to_blocks(a):
        return jnp.moveaxis(a.reshape(a.shape[0], nb, qb, *a.shape[2:]), 1, 0)

    out = lax.map(lambda blk: attend(*blk), tuple(to_blocks(q) for q in qs) + (q_pos.reshape(nb, qb),))
    out = jnp.moveaxis(out, 0, 1)
    return out.reshape(out.shape[0], n_q, *out.shape[3:])


def mla_block(q_lat, q_rope, q_pos, ckv, k_rope, k_pos):
    s = jnp.einsum('bqhr,bkr->bhqk', q_lat, ckv) + jnp.einsum('bqhd,bkd->bhqk', q_rope, k_rope)
    s = s.astype(jnp.float32) * MLA_SCALE
    s = jnp.where(k_pos[None, :] <= q_pos[:, None], s, -jnp.inf)
    p = jax.nn.softmax(s, axis=-1).astype(ckv.dtype)
    return jnp.einsum('bhqk,bkr->bqhr', p, ckv)


def stick_breaking_block(q, q_pos, k, v, k_pos):
    b, nq, _, dh = q.shape
    qg = q.reshape(b, nq, SB_KV_HEADS, SB_REP, dh)
    z = jnp.einsum('bqgrd,bkgd->bgrqk', qg, k).astype(jnp.float32) * SB_SCALE
    valid = k_pos[None, :] < q_pos[:, None]
    log_beta = jax.nn.log_sigmoid(z)
    log_keep = jnp.where(valid, jax.nn.log_sigmoid(-z), 0.0)
    suffix = lax.cumsum(log_keep, axis=4, reverse=True) - log_keep
    a = jnp.where(valid, jnp.exp(log_beta + suffix), 0.0).astype(v.dtype)
    o = jnp.einsum('bgrqk,bkgd->bqgrd', a, v)
    return o.reshape(b, nq, SB_HEADS, dh)


def token_mixer(x, ckv_past, krope_past, sbk_past, sbv_past, conv_buf,
                w_in, q_norm, kv_norm, w_uq, w_uk, w_uv, conv_w,
                w_br_mla, w_br_conv, w_br_sb, w_o):
    b, s, d = x.shape
    past = ckv_past.shape[1]
    q_pos = past + jnp.arange(s)
    k_pos = jnp.arange(past + s)
    (c_q, c_kv, k_r, conv_b, conv_c, conv_x, sb_q, sb_k, sb_v, gates) = split_cols(x @ w_in)

    q = jnp.einsum('bsr,rhd->bshd', rms_norm(c_q, q_norm), w_uq)
    q_nope = q[..., :MLA_NOPE_DIM]
    q_rope = rope(q[..., MLA_NOPE_DIM:], q_pos)
    q_lat = jnp.einsum('bshn,rhn->bshr', q_nope, w_uk)
    ckv_new = rms_norm(c_kv, kv_norm)
    krope_new = rope(k_r, q_pos)
    ckv_all = jnp.concatenate([ckv_past, ckv_new], axis=1)
    krope_all = jnp.concatenate([krope_past, krope_new], axis=1)
    o_lat = sweep_query_blocks(lambda ql, qr, qp: mla_block(ql, qr, qp, ckv_all, krope_all, k_pos),
                               (q_lat, q_rope), q_pos)
    o_mla = jnp.einsum('bshr,rhv->bshv', o_lat, w_uv).reshape(b, s, MLA_HEADS * MLA_V_DIM)

    u_all = jnp.concatenate([conv_buf, conv_c * conv_x], axis=1)
    conv_out = sum(conv_w[j] * u_all[:, j:j + s] for j in range(CONV_WIDTH))
    o_conv = conv_b * conv_out
    conv_new = u_all[:, s:]

    q_sb = sb_q.reshape(b, s, SB_HEADS, SB_HEAD_DIM)
    k_new = sb_k.reshape(b, s, SB_KV_HEADS, SB_HEAD_DIM)
    v_new = sb_v.reshape(b, s, SB_KV_HEADS, SB_HEAD_DIM)
    k_all = jnp.concatenate([sbk_past, k_new], axis=1)
    v_all = jnp.concatenate([sbv_past, v_new], axis=1)
    o_sb = sweep_query_blocks(lambda qq, qp: stick_breaking_block(qq, qp, k_all, v_all, k_pos),
                              (q_sb,), q_pos).reshape(b, s, SB_HEADS * SB_HEAD_DIM)

    g = jax.nn.sigmoid(gates.astype(jnp.float32)).astype(x.dtype).reshape(b, s, N_BRANCHES, d)
    merged = (g[:, :, 0] * (o_mla @ w_br_mla) + g[:, :, 1] * (o_conv @ w_br_conv)
              + g[:, :, 2] * (o_sb @ w_br_sb))
    return merged @ w_o, ckv_new, krope_new, k_new, v_new, conv_new


def moe_ffn(x2d, router_w, router_b, w_gate, b_gate, w_up, b_up, w_down, b_down):
    n_tok, d = x2d.shape
    logits = (x2d @ router_w).astype(jnp.float32) + router_b.astype(jnp.float32)
    top_logit, top_idx = lax.top_k(logits, TOP_K)
    gate_w = jax.nn.softmax(top_logit, axis=-1).astype(x2d.dtype)
    n_assign = n_tok * TOP_K
    flat_e = top_idx.reshape(n_assign)
    order = jnp.argsort(flat_e)
    sorted_e = flat_e[order]
    counts = jnp.bincount(flat_e, length=N_EXPERTS)
    starts = jnp.cumsum(counts) - counts
    padded = (counts + MOE_BLOCK - 1) // MOE_BLOCK * MOE_BLOCK
    pad_end = jnp.cumsum(padded)
    pad_start = pad_end - padded
    pos_sorted = pad_start[sorted_e] + jnp.arange(n_assign) - starts[sorted_e]
    n_blocks = -(-(n_assign + N_EXPERTS * (MOE_BLOCK - 1)) // MOE_BLOCK)
    x_pad = jnp.zeros((n_blocks * MOE_BLOCK, d), x2d.dtype).at[pos_sorted].set(x2d[order // TOP_K])
    block_e = jnp.minimum(jnp.searchsorted(pad_end, jnp.arange(n_blocks) * MOE_BLOCK, side='right'),
                          N_EXPERTS - 1)

    def expert_block(args):
        xb, e = args
        gt = jnp.minimum(xb @ w_gate[e] + b_gate[e], SWIGLU_LIMIT)
        up = jnp.clip(xb @ w_up[e] + b_up[e], -SWIGLU_LIMIT, SWIGLU_LIMIT)
        h = gt * jax.nn.sigmoid(SWIGLU_ALPHA * gt) * (up + 1.0)
        return h @ w_down[e] + b_down[e]

    y_pad = lax.map(expert_block, (x_pad.reshape(n_blocks, MOE_BLOCK, d), block_e))
    y_pad = y_pad.reshape(n_blocks * MOE_BLOCK, d)
    pos = jnp.zeros((n_assign,), pos_sorted.dtype).at[order].set(pos_sorted)
    return jnp.einsum('tk,tkd->td', gate_w, y_pad[pos].reshape(n_tok, TOP_K, d))


def trunk_layer(x, ckv_past, krope_past, sbk_past, sbv_past, conv_buf,
                w_in, q_norm, kv_norm, w_uq, w_uk, w_uv, conv_w,
                w_br_mla, w_br_conv, w_br_sb, w_o, ln1_g, ln1_b,
                router_w, router_b, w_gate, b_gate, w_up, b_up, w_down, b_down, ln2_g, ln2_b):
    b, s, d = x.shape
    mix, ckv_new, krope_new, k_new, v_new, conv_new = token_mixer(
        x, ckv_past, krope_past, sbk_past, sbv_past, conv_buf,
        w_in, q_norm, kv_norm, w_uq, w_uk, w_uv, conv_w, w_br_mla, w_br_conv, w_br_sb, w_o)
    x = layer_norm(DEEPNORM_ALPHA * x + mix, ln1_g, ln1_b)
    ffn = moe_ffn(x.reshape(b * s, d), router_w, router_b, w_gate, b_gate, w_up, b_up,
                  w_down, b_down).reshape(b, s, d)
    x = layer_norm(DEEPNORM_ALPHA * x + ffn, ln2_g, ln2_b)
    return x, ckv_new, krope_new, k_new, v_new, conv_new


def setup_inputs(seed: int = 0) -> dict:
    key = jax.random.key(seed)
    ks = iter(jax.random.split(key, 40))
    f32 = jnp.float32

    def rnd(shape, scale):
        return jax.random.normal(next(ks), shape, f32) * scale

    def gain(shape):
        return 1.0 + rnd(shape, 0.01)

    n_pages = PAST_LEN // PAGE_SIZE
    n_pool = (DEC_BATCH * n_pages * 5) // 4
    beta = DEEPNORM_BETA
    x_prompt = rnd((BATCH, SEQ, D_MODEL), 1.0)
    x_sample = rnd((DEC_BATCH, DEC_SEQ, D_MODEL), 1.0)
    cache_mla_ckv = rnd((DEPTH, n_pool, PAGE_SIZE, MLA_KV_RANK), 1.0)
    cache_mla_krope = rnd((DEPTH, n_pool, PAGE_SIZE, MLA_ROPE_DIM), 1.0)
    cache_sb_k = rnd((DEPTH, n_pool, PAGE_SIZE, SB_KV_HEADS, SB_HEAD_DIM), 1.0)
    cache_sb_v = rnd((DEPTH, n_pool, PAGE_SIZE, SB_KV_HEADS, SB_HEAD_DIM), 1.0)
    state_conv = rnd((DEPTH, DEC_BATCH, CONV_WIDTH - 1, CONV_DIM), 1.0)
    page_table = jax.random.permutation(next(ks), n_pool)[: DEC_BATCH * n_pages]
    page_table = page_table.reshape(DEC_BATCH, n_pages).astype(jnp.int32)
    return {
        'x_prompt': x_prompt,
        'x_sample': x_sample,
        'cache_mla_ckv': cache_mla_ckv,
        'cache_mla_krope': cache_mla_krope,
        'cache_sb_k': cache_sb_k,
        'cache_sb_v': cache_sb_v,
        'state_conv': state_conv,
        'page_table': page_table,
        'w_in': rnd((DEPTH, D_MODEL, IN_COLS), D_MODEL ** -0.5),
        'mla_q_norm': gain((DEPTH, MLA_Q_RANK)),
        'mla_kv_norm': gain((DEPTH, MLA_KV_RANK)),
        'mla_w_uq': rnd((DEPTH, MLA_Q_RANK, MLA_HEADS, MLA_NOPE_DIM + MLA_ROPE_DIM), MLA_Q_RANK ** -0.5),
        'mla_w_uk': rnd((DEPTH, MLA_KV_RANK, MLA_HEADS, MLA_NOPE_DIM), MLA_KV_RANK ** -0.5),
        'mla_w_uv': rnd((DEPTH, MLA_KV_RANK, MLA_HEADS, MLA_V_DIM), MLA_KV_RANK ** -0.5),
        'conv_w': rnd((DEPTH, CONV_WIDTH, CONV_DIM), CONV_WIDTH ** -0.5),
        'w_branch_mla': rnd((DEPTH, MLA_HEADS * MLA_V_DIM, D_MODEL), beta * BRANCH_DIM ** -0.5),
        'w_branch_conv': rnd((DEPTH, CONV_DIM, D_MODEL), beta * BRANCH_DIM ** -0.5),
        'w_branch_sb': rnd((DEPTH, SB_HEADS * SB_HEAD_DIM, D_MODEL), beta * BRANCH_DIM ** -0.5),
        'w_o': rnd((DEPTH, D_MODEL, D_MODEL), beta * D_MODEL ** -0.5),
        'ln1_g': gain((DEPTH, D_MODEL)),
        'ln1_b': rnd((DEPTH, D_MODEL), 0.01),
        'router_w': rnd((DEPTH, D_MODEL, N_EXPERTS), D_MODEL ** -0.5),
        'router_b': rnd((DEPTH, N_EXPERTS), 0.01),
        'exp_w_gate': rnd((DEPTH, N_EXPERTS, D_MODEL, D_FF), D_MODEL ** -0.5),
        'exp_b_gate': rnd((DEPTH, N_EXPERTS, D_FF), 0.01),
        'exp_w_up': rnd((DEPTH, N_EXPERTS, D_MODEL, D_FF), D_MODEL ** -0.5),
        'exp_b_up': rnd((DEPTH, N_EXPERTS, D_FF), 0.01),
        'exp_w_down': rnd((DEPTH, N_EXPERTS, D_FF, D_MODEL), beta * D_FF ** -0.5),
        'exp_b_down': rnd((DEPTH, N_EXPERTS, D_MODEL), 0.01),
        'ln2_g': gain((DEPTH, D_MODEL)),
        'ln2_b': rnd((DEPTH, D_MODEL), 0.01),
    }


def reference(x_prompt, x_sample, cache_mla_ckv, cache_mla_krope, cache_sb_k, cache_sb_v, state_conv,
              page_table, w_in, mla_q_norm, mla_kv_norm, mla_w_uq, mla_w_uk, mla_w_uv, conv_w,
              w_branch_mla, w_branch_conv, w_branch_sb, w_o, ln1_g, ln1_b, router_w, router_b,
              exp_w_gate, exp_b_gate, exp_w_up, exp_b_up, exp_w_down, exp_b_down, ln2_g, ln2_b):
    dt = x_prompt.dtype
    bp = x_prompt.shape[0]
    yp, ys = x_prompt, x_sample
    new_p = [[], [], [], [], []]
    new_s = [[], [], [], [], []]
    for l in range(DEPTH):
        lw = (w_in[l], mla_q_norm[l], mla_kv_norm[l], mla_w_uq[l], mla_w_uk[l], mla_w_uv[l], conv_w[l],
              w_branch_mla[l], w_branch_conv[l], w_branch_sb[l], w_o[l], ln1_g[l], ln1_b[l],
              router_w[l], router_b[l], exp_w_gate[l], exp_b_gate[l], exp_w_up[l], exp_b_up[l],
              exp_w_down[l], exp_b_down[l], ln2_g[l], ln2_b[l])
        yp, *st = trunk_layer(
            yp,
            jnp.zeros((bp, 0, MLA_KV_RANK), dt),
            jnp.zeros((bp, 0, MLA_ROPE_DIM), dt),
            jnp.zeros((bp, 0, SB_KV_HEADS, SB_HEAD_DIM), dt),
            jnp.zeros((bp, 0, SB_KV_HEADS, SB_HEAD_DIM), dt),
            jnp.zeros((bp, CONV_WIDTH - 1, CONV_DIM), dt),
            *lw)
        for lst, a in zip(new_p, st):
            lst.append(a)
        ys, *st = trunk_layer(
            ys,
            gather_pages(cache_mla_ckv, l, page_table),
            gather_pages(cache_mla_krope, l, page_table),
            gather_pages(cache_sb_k, l, page_table),
            gather_pages(cache_sb_v, l, page_table),
            state_conv[l],
            *lw)
        for lst, a in zip(new_s, st):
            lst.append(a)
    ckv_prompt, krope_prompt, sbk_prompt, sbv_prompt, conv_prompt = [jnp.stack(a) for a in new_p]
    ckv_sample, krope_sample, sbk_sample, sbv_sample, conv_sample = [jnp.stack(a) for a in new_s]
    return (yp, ys, ckv_prompt, krope_prompt, sbk_prompt, sbv_prompt, conv_prompt,
            ckv_sample, krope_sample, sbk_sample, sbv_sample, conv_sample)
```

```python
import functools

import jax
import jax.numpy as jnp
from jax import lax
from jax.experimental import pallas as pl
from jax.experimental.pallas import tpu as pltpu

F32 = jnp.float32
BF16 = jnp.bfloat16
U32 = jnp.uint32

D_MODEL = 1024
PAGE = 128
MLA_H = 8
MLA_DN = 64
MLA_DR = 32
MLA_DV = 64
MLA_QR = 384
MLA_KVR = 256
MLA_SCALE = (MLA_DN + MLA_DR) ** -0.5
ROPE_THETA = 10000.0
CONV_DIM = 512
SB_H = 8
SB_G = 2
SB_D = 64
SB_SCALE = SB_D ** -0.5
N_EXPERTS = 32
TOP_K = 4
SWIGLU_LIMIT = 7.0
SWIGLU_ALPHA = 1.702
LN_EPS = 1e-5
RMS_EPS = 1e-6
LANES = 128
HEAD_PAD = 128
MOE_TILE = 256
NEG_BIG = -1e30
SB_DEAD_LOG = -120.0

C_GATE, C_SBQ, C_CB, C_CC, C_CX, C_CKV, C_SBK, C_SBV, C_KRA, C_KRB, C_CQ = (
    0, 3072, 4096, 4608, 5120, 5632, 5888, 6016, 6144, 6272, 6528)
H_COLS = 6912
VMEM_LIMIT = 56 * 1024 * 1024


def _cp(sem, vmem=VMEM_LIMIT):
    return pltpu.CompilerParams(dimension_semantics=sem, vmem_limit_bytes=vmem)


def _linear_kernel(x_ref, w_ref, o_ref, xb_ref):
    @pl.when(pl.program_id(1) == 0)
    def _():
        xb_ref[...] = x_ref[...].astype(BF16)

    o_ref[...] = jnp.dot(xb_ref[...], w_ref[...], preferred_element_type=F32).astype(o_ref.dtype)


def _linear(x, w, out_dtype, tm, tn):
    m, k = x.shape
    n = w.shape[1]
    return pl.pallas_call(
        _linear_kernel,
        grid=(m // tm, n // tn),
        in_specs=[pl.BlockSpec((tm, k), lambda i, j: (i, 0)),
                  pl.BlockSpec((k, tn), lambda i, j: (0, j))],
        out_specs=pl.BlockSpec((tm, tn), lambda i, j: (i, j)),
        out_shape=jax.ShapeDtypeStruct((m, n), out_dtype),
        scratch_shapes=[pltpu.VMEM((tm, k), BF16)],
        compiler_params=_cp(("arbitrary", "arbitrary")),
        name="linear",
    )(x, w)


def _headwise_kernel(x_ref, w_ref, o_ref):
    o_ref[...] = jnp.dot(x_ref[...], w_ref[0], preferred_element_type=F32).astype(o_ref.dtype)


def _headwise_linear(x, w, out_dtype):
    m = x.shape[0]
    h, k, n = w.shape
    return pl.pallas_call(
        _headwise_kernel,
        grid=(h,),
        in_specs=[pl.BlockSpec((m, k), lambda i: (0, i)),
                  pl.BlockSpec((1, k, n), lambda i: (i, 0, 0))],
        out_specs=pl.BlockSpec((m, n), lambda i: (0, i)),
        out_shape=jax.ShapeDtypeStruct((m, h * n), out_dtype),
        compiler_params=_cp(("arbitrary",)),
        name="headwise_linear",
    )(x, w)


def _rms(x, g):
    return x * lax.rsqrt(jnp.mean(x * x, axis=-1, keepdims=True) + RMS_EPS) * g


def _prep_kernel(cq_ref, ckv_ref, kra_ref, krb_ref, cb_ref, cc_ref, cx_ref, pa_ref, pb_ref,
                 sq_ref, sk_ref, sv_ref, cos_ref, sin_ref, qn_ref, kvn_ref, wuq_ref, wuk_ref, wuv_ref, cw_ref,
                 ckv_o, kr_o, q_o, k_o, v_o, oc_o, tail_o, sq_o, sk_o, sv_o, *scratch, decode, tiles_per_seq):
    tm = cq_ref.shape[0]
    cos = cos_ref[...]
    sin = sin_ref[...]
    cos8 = jnp.concatenate([cos] * MLA_H, axis=1)
    sin8 = jnp.concatenate([sin] * MLA_H, axis=1)

    cqn = _rms(cq_ref[...], qn_ref[...]).astype(BF16)
    qq = jnp.dot(cqn, wuq_ref[...], preferred_element_type=F32)
    width = MLA_H * HEAD_PAD
    q_o[...] = (qq[:, :width] * cos8 + qq[:, width:] * sin8).astype(BF16)

    ckv = _rms(ckv_ref[...], kvn_ref[...])
    ckv_o[...] = ckv
    ckvb = ckv.astype(BF16)
    krp = kra_ref[...] * cos + krb_ref[...] * sin
    kr_o[...] = krp[:, MLA_DN:MLA_DN + MLA_DR]
    kn = jnp.dot(ckvb, wuk_ref[...], preferred_element_type=F32)
    k_o[...] = (kn + jnp.concatenate([krp] * MLA_H, axis=1)).astype(BF16)
    v_o[...] = jnp.dot(ckvb, wuv_ref[...], preferred_element_type=F32).astype(BF16)

    u = cc_ref[...] * cx_ref[...]
    w0 = cw_ref[0:1, :]
    w1 = cw_ref[1:2, :]
    w2 = cw_ref[2:3, :]
    if decode:
        conv = w0 * pa_ref[...] + w1 * pb_ref[...] + w2 * u
        tail_o[...] = u
    else:
        ext_ref, = scratch
        first = (pl.program_id(0) % tiles_per_seq) == 0
        prev = pa_ref[...] * pb_ref[...]
        ext_ref[0:8, :] = jnp.where(first, 0.0, prev)
        ext_ref[8:, :] = u
        conv = w0 * ext_ref[pl.ds(6, tm), :] + w1 * ext_ref[pl.ds(7, tm), :] + w2 * u
        tail_o[...] = u[tm - 8:, :]
    oc_o[...] = (cb_ref[...] * conv).astype(BF16)

    sq_o[...] = sq_ref[...].astype(BF16)
    sk_o[...] = sk_ref[...].astype(BF16)
    sv_o[...] = sv_ref[...].astype(BF16)


def _prep(h, cos_t, sin_t, qn, kvn, wuq, wuk, wuv, cw, state, *, tm, decode, tiles_per_seq):
    t = h.shape[0]
    nt = t // tm
    n_tab = cos_t.shape[0] // tm

    def hs(width, col):
        return pl.BlockSpec((tm, width), lambda i: (i, col // width))

    def full(a):
        return pl.BlockSpec(a.shape, lambda i: (0,) * a.ndim)

    if decode:
        pa, pb = state
        pa_spec = pb_spec = pl.BlockSpec((tm, CONV_DIM), lambda i: (i, 0))
        tail_shape, tail_spec = (t, CONV_DIM), pl.BlockSpec((tm, CONV_DIM), lambda i: (i, 0))
        scratch = []
    else:
        pa = pb = h
        r8 = tm // 8
        pa_spec = pl.BlockSpec((8, CONV_DIM), lambda i: (jnp.maximum(i * r8 - 1, 0), C_CC // CONV_DIM))
        pb_spec = pl.BlockSpec((8, CONV_DIM), lambda i: (jnp.maximum(i * r8 - 1, 0), C_CX // CONV_DIM))
        tail_shape, tail_spec = (nt * 8, CONV_DIM), pl.BlockSpec((8, CONV_DIM), lambda i: (i, 0))
        scratch = [pltpu.VMEM((tm + 8, CONV_DIM), F32)]
    tab_spec = pl.BlockSpec((tm, LANES), lambda i: (i % n_tab, 0))
    wide = MLA_H * HEAD_PAD
    out_shape = [
        jax.ShapeDtypeStruct((t, MLA_KVR), F32), jax.ShapeDtypeStruct((t, MLA_DR), F32),
        jax.ShapeDtypeStruct((t, wide), BF16), jax.ShapeDtypeStruct((t, wide), BF16),
        jax.ShapeDtypeStruct((t, wide), BF16), jax.ShapeDtypeStruct((t, CONV_DIM), BF16),
        jax.ShapeDtypeStruct(tail_shape, F32), jax.ShapeDtypeStruct((t, SB_H * HEAD_PAD), BF16),
        jax.ShapeDtypeStruct((t, LANES), BF16), jax.ShapeDtypeStruct((t, LANES), BF16)]

    def os(width):
        return pl.BlockSpec((tm, width), lambda i: (i, 0))

    out_specs = [os(MLA_KVR), os(MLA_DR), os(wide), os(wide), os(wide), os(CONV_DIM), tail_spec,
                 os(SB_H * HEAD_PAD), os(LANES), os(LANES)]
    return pl.pallas_call(
        functools.partial(_prep_kernel, decode=decode, tiles_per_seq=tiles_per_seq),
        grid=(nt,),
        in_specs=[hs(MLA_QR, C_CQ), hs(MLA_KVR, C_CKV), hs(LANES, C_KRA), hs(LANES, C_KRB),
                  hs(CONV_DIM, C_CB), hs(CONV_DIM, C_CC), hs(CONV_DIM, C_CX), pa_spec, pb_spec,
                  hs(SB_H * HEAD_PAD, C_SBQ), hs(LANES, C_SBK), hs(LANES, C_SBV),
                  tab_spec, tab_spec, full(qn), full(kvn), full(wuq), full(wuk), full(wuv), full(cw)],
        out_specs=out_specs,
        out_shape=out_shape,
        scratch_shapes=scratch,
        compiler_params=_cp(("arbitrary",)),
        name="prep_decode" if decode else "prep",
    )(h, h, h, h, h, h, h, pa, pb, h, h, h, cos_t, sin_t, qn, kvn, wuq, wuk, wuv, cw)


def _mla_attn_kernel(q_ref, k_ref, v_ref, o_ref, *, tq):
    qi = pl.program_id(2)
    q = q_ref[...]

    def scores(j):
        off = pl.multiple_of(j * tq, tq)
        k = k_ref[pl.ds(off, tq), :]
        v = v_ref[pl.ds(off, tq), :]
        s = lax.dot_general(q, k, (((1,), (1,)), ((), ())), preferred_element_type=F32) * MLA_SCALE
        return s, v

    def update(carry, s, v):
        m, l, acc = carry
        m_new = jnp.maximum(m, jnp.max(s, axis=-1, keepdims=True))
        alpha = jnp.exp(m - m_new)
        p = jnp.exp(s - m_new)
        l = alpha * l + jnp.sum(p, axis=-1, keepdims=True)
        acc = alpha * acc + jnp.dot(p.astype(BF16), v, preferred_element_type=F32)
        return m_new, l, acc

    def body(j, carry):
        s, v = scores(j)
        return update(carry, s, v)

    init = (jnp.full((tq, 1), NEG_BIG, F32), jnp.zeros((tq, 1), F32), jnp.zeros((tq, HEAD_PAD), F32))
    carry = lax.fori_loop(0, qi, body, init)
    s, v = scores(qi)
    row = lax.broadcasted_iota(jnp.int32, (tq, tq), 0)
    col = lax.broadcasted_iota(jnp.int32, (tq, tq), 1)
    s = jnp.where(col <= row, s, NEG_BIG)
    _, l, acc = update(carry, s, v)
    o_ref[...] = (acc / l).astype(o_ref.dtype)


def _mla_attn(q, k, v, *, batch, seq, tq):
    nq = seq // tq
    qspec = pl.BlockSpec((tq, HEAD_PAD), lambda b, h, i: (b * nq + i, h))
    kspec = pl.BlockSpec((seq, HEAD_PAD), lambda b, h, i: (b, h))
    return pl.pallas_call(
        functools.partial(_mla_attn_kernel, tq=tq),
        grid=(batch, MLA_H, nq),
        in_specs=[qspec, kspec, kspec],
        out_specs=qspec,
        out_shape=jax.ShapeDtypeStruct(q.shape, BF16),
        compiler_params=_cp(("arbitrary", "arbitrary", "arbitrary")),
        name="mla_attn",
    )(q, k, v)


def _log_sigmoid(z):
    return jnp.minimum(z, 0.0) - jnp.log1p(jnp.exp(-jnp.abs(z)))


def _strict_suffix(lk, upper):
    hi = lk.astype(BF16)
    lo = (lk - hi.astype(F32)).astype(BF16)
    return (jnp.dot(hi, upper, preferred_element_type=F32) + jnp.dot(lo, upper, preferred_element_type=F32))


def _upper(n):
    j = lax.broadcasted_iota(jnp.int32, (n, n), 0)
    s = lax.broadcasted_iota(jnp.int32, (n, n), 1)
    return jnp.where(j > s, 1.0, 0.0).astype(BF16)


def _sb_attn_kernel(q_ref, k_ref, v_ref, o_ref, *, tq):
    qi = pl.program_id(2)
    q = q_ref[...]
    upper = _upper(tq)

    def block(j, carry, acc, diag):
        off = pl.multiple_of(j * tq, tq)
        k = k_ref[pl.ds(off, tq), :]
        v = v_ref[pl.ds(off, tq), :]
        z = lax.dot_general(q, k, (((1,), (1,)), ((), ())), preferred_element_type=F32) * SB_SCALE
        lb = _log_sigmoid(z)
        lk = lb - z
        if diag:
            row = lax.broadcasted_iota(jnp.int32, (tq, tq), 0)
            col = lax.broadcasted_iota(jnp.int32, (tq, tq), 1)
            valid = col < row
            lk = jnp.where(valid, lk, 0.0)
        a = jnp.exp(lb + _strict_suffix(lk, upper) + carry)
        if diag:
            a = jnp.where(valid, a, 0.0)
        acc = acc + jnp.dot(a.astype(BF16), v, preferred_element_type=F32)
        carry = carry + jnp.sum(lk, axis=-1, keepdims=True)
        return carry, acc

    carry, acc = block(qi, jnp.zeros((tq, 1), F32), jnp.zeros((tq, HEAD_PAD), F32), True)

    def cond(st):
        return jnp.logical_and(st[0] < qi, jnp.max(st[1]) > SB_DEAD_LOG)

    def body(st):
        t, carry, acc = st
        carry, acc = block(qi - 1 - t, carry, acc, False)
        return t + 1, carry, acc

    _, carry, acc = lax.while_loop(cond, body, (jnp.int32(0), carry, acc))
    o_ref[...] = acc.astype(o_ref.dtype)


def _sb_attn(q, k, v, *, batch, seq, tq):
    nq = seq // tq
    qspec = pl.BlockSpec((tq, HEAD_PAD), lambda b, h, i: (b * nq + i, h))
    kspec = pl.BlockSpec((seq, LANES), lambda b, h, i: (b, 0))
    return pl.pallas_call(
        functools.partial(_sb_attn_kernel, tq=tq),
        grid=(batch, SB_H, nq),
        in_specs=[qspec, kspec, kspec],
        out_specs=qspec,
        out_shape=jax.ShapeDtypeStruct(q.shape, BF16),
        compiler_params=_cp(("arbitrary", "arbitrary", "arbitrary")),
        name="sb_attn",
    )(q, k, v)


def _decode_kernel(pt_ref, lay_ref, qe_ref, qs_ref, cn_ref, kn_ref, *refs, pages_per_step, n_steps):
    del pt_ref, lay_ref
    p_ = pages_per_step
    caches = refs[:4 * p_]
    olat_ref, osb_ref, m_ref, l_ref, acc_ref, car_ref, accs_ref = refs[4 * p_:]
    j = pl.program_id(1)
    qe = qe_ref[0]
    ql = qe[:, :MLA_KVR]
    qr = qe[:, MLA_KVR:MLA_KVR + MLA_DR]
    qs = qs_ref[0]

    @pl.when(j == 0)
    def _():
        cn = cn_ref[0]
        kn = kn_ref[0]
        s0 = (jnp.sum(ql.astype(F32) * cn, axis=-1, keepdims=True)
              + jnp.sum(qr.astype(F32) * kn, axis=-1, keepdims=True)) * MLA_SCALE
        m_ref[...] = s0
        l_ref[...] = jnp.ones_like(l_ref)
        acc_ref[...] = jnp.broadcast_to(cn, acc_ref.shape)
        car_ref[...] = jnp.zeros_like(car_ref)
        accs_ref[...] = jnp.zeros_like(accs_ref)

    upper = _upper(PAGE)
    contract_last = (((1,), (1,)), ((), ()))
    for i in range(p_):
        ckv = caches[i][0, 0].astype(BF16)
        kr_t = caches[p_ + i][0, 0].astype(BF16)
        s = (lax.dot_general(ql, ckv, contract_last, preferred_element_type=F32)
             + jnp.dot(qr, kr_t, preferred_element_type=F32)) * MLA_SCALE
        m = m_ref[...]
        m_new = jnp.maximum(m, jnp.max(s, axis=-1, keepdims=True))
        alpha = jnp.exp(m - m_new)
        p = jnp.exp(s - m_new)
        l_ref[...] = alpha * l_ref[...] + jnp.sum(p, axis=-1, keepdims=True)
        acc_ref[...] = alpha * acc_ref[...] + jnp.dot(p.astype(BF16), ckv, preferred_element_type=F32)
        m_ref[...] = m_new

        @pl.when(jnp.max(car_ref[...]) > SB_DEAD_LOG)
        def _(i=i):
            k_t = caches[2 * p_ + i][0, 0].astype(BF16)
            v_t = caches[3 * p_ + i][0, 0].astype(BF16)
            z = jnp.dot(qs, k_t, preferred_element_type=F32) * SB_SCALE
            lb = _log_sigmoid(z)
            lk = lb - z
            a = jnp.exp(lb + _strict_suffix(lk, upper) + car_ref[...])
            accs_ref[...] += lax.dot_general(a.astype(BF16), v_t, contract_last, preferred_element_type=F32)
            car_ref[...] += jnp.sum(lk, axis=-1, keepdims=True)

    @pl.when(j == n_steps - 1)
    def _():
        olat_ref[0] = (acc_ref[...] / l_ref[...]).astype(olat_ref.dtype)
        osb_ref[0] = accs_ref[...].astype(osb_ref.dtype)


def _decode_attn(page_table, layer, qe, qs, ckv_new, kr_new, c_ckv, c_krt, c_kt, c_vt, *, pages_per_step):
    nb = qe.shape[0]
    n_pages = page_table.shape[0] // nb
    p_ = pages_per_step
    n_steps = n_pages // p_

    def cache_spec(shape, i):
        def imap(b, j, pt, lay):
            return (lay[0], pt[b * n_pages + (n_pages - 1 - (j * p_ + i))], 0, 0)
        return pl.BlockSpec((1, 1) + shape, imap)

    def seq_spec(shape):
        return pl.BlockSpec((1,) + shape, lambda b, j, pt, lay: (b, 0, 0))

    cache_specs, cache_args = [], []
    for arr in (c_ckv, c_krt, c_kt, c_vt):
        for i in range(p_):
            cache_specs.append(cache_spec(arr.shape[2:], i))
            cache_args.append(arr)
    grid_spec = pltpu.PrefetchScalarGridSpec(
        num_scalar_prefetch=2,
        grid=(nb, n_steps),
        in_specs=[seq_spec(qe.shape[1:]), seq_spec(qs.shape[1:]), seq_spec((1, MLA_KVR)), seq_spec((1, MLA_DR))]
        + cache_specs,
        out_specs=[seq_spec((MLA_H, MLA_KVR)), seq_spec((SB_H, HEAD_PAD))],
        scratch_shapes=[pltpu.VMEM((MLA_H, 1), F32), pltpu.VMEM((MLA_H, 1), F32), pltpu.VMEM((MLA_H, MLA_KVR), F32),
                        pltpu.VMEM((SB_H, 1), F32), pltpu.VMEM((SB_H, HEAD_PAD), F32)])
    return pl.pallas_call(
        functools.partial(_decode_kernel, pages_per_step=p_, n_steps=n_steps),
        grid_spec=grid_spec,
        out_shape=[jax.ShapeDtypeStruct((nb, MLA_H, MLA_KVR), BF16), jax.ShapeDtypeStruct((nb, SB_H, HEAD_PAD), BF16)],
        compiler_params=_cp(("arbitrary", "arbitrary")),
        name="decode_attn",
    )(page_table, layer, qe, qs, ckv_new, kr_new, *cache_args)


def _layer_norm(x, g, b):
    mu = jnp.mean(x, axis=-1, keepdims=True)
    xc = x - mu
    var = jnp.mean(xc * xc, axis=-1, keepdims=True)
    return xc * lax.rsqrt(var + LN_EPS) * g + b


def _merge_kernel(om_ref, oc_ref, os_ref, g0_ref, g1_ref, g2_ref, x_ref, wbm_ref, wbc_ref, wbs_ref, wo_ref,
                  lg_ref, lb_ref, rwh_ref, rwl_ref, rb_ref, x1_ref, xp_ref, idx_ref, gw_ref, *, alpha):
    tm = x_ref.shape[0]
    merged = (jax.nn.sigmoid(g0_ref[...]) * jnp.dot(om_ref[...], wbm_ref[...], preferred_element_type=F32)
              + jax.nn.sigmoid(g1_ref[...]) * jnp.dot(oc_ref[...], wbc_ref[...], preferred_element_type=F32)
              + jax.nn.sigmoid(g2_ref[...]) * jnp.dot(os_ref[...], wbs_ref[...], preferred_element_type=F32))
    mix = jnp.dot(merged.astype(BF16), wo_ref[...], preferred_element_type=F32)
    x1 = _layer_norm(alpha * x_ref[...] + mix, lg_ref[...], lb_ref[...])
    x1_ref[...] = x1
    xb = x1.astype(BF16)

    bits = pltpu.bitcast(xb.astype(F32), U32)
    half = D_MODEL // 2
    word = bits[:, half:] | (bits[:, :half] >> 16)
    for c in range(half // LANES):
        xp_ref[pl.ds(c, tm, stride=half // LANES), :] = word[:, c * LANES:(c + 1) * LANES]

    xl = (x1 - xb.astype(F32)).astype(BF16)
    logits = (jnp.dot(xb, rwh_ref[...], preferred_element_type=F32) + jnp.dot(xl, rwh_ref[...], preferred_element_type=F32)
              + jnp.dot(xb, rwl_ref[...], preferred_element_type=F32) + rb_ref[...])
    lane = lax.broadcasted_iota(jnp.int32, logits.shape, 1)
    vals, idxs = [], []
    for _ in range(TOP_K):
        mx = jnp.max(logits, axis=-1, keepdims=True)
        ix = jnp.min(jnp.where(logits == mx, lane, LANES), axis=-1, keepdims=True)
        vals.append(mx)
        idxs.append(ix)
        logits = jnp.where(lane == ix, NEG_BIG * 2, logits)
    es = [jnp.exp(v - vals[0]) for v in vals]
    den = es[0] + es[1] + es[2] + es[3]
    lane8 = lax.broadcasted_iota(jnp.int32, (tm, 8), 1)
    idx_out = jnp.zeros((tm, 8), jnp.int32)
    gw_out = jnp.zeros((tm, 8), F32)
    for kk in range(TOP_K):
        idx_out = jnp.where(lane8 == kk, idxs[kk], idx_out)
        gw_out = jnp.where(lane8 == kk, es[kk] / den, gw_out)
    idx_ref[...] = idx_out
    gw_ref[...] = gw_out


def _merge(om, oc, osb, h, x, wbm, wbc, wbs, wo, lg, lb, rwh, rwl, rb, *, tm, alpha):
    t = x.shape[0]

    def rs(width, col=0):
        return pl.BlockSpec((tm, width), lambda i: (i, col))

    def full(a):
        return pl.BlockSpec(a.shape, lambda i: (0,) * a.ndim)

    pack_rows = D_MODEL // 2 // LANES
    return pl.pallas_call(
        functools.partial(_merge_kernel, alpha=alpha),
        grid=(t // tm,),
        in_specs=[rs(om.shape[1]), rs(CONV_DIM), rs(osb.shape[1]), rs(D_MODEL, 0), rs(D_MODEL, 1), rs(D_MODEL, 2),
                  rs(D_MODEL), full(wbm), full(wbc), full(wbs), full(wo), full(lg), full(lb), full(rwh), full(rwl), full(rb)],
        out_specs=[rs(D_MODEL), pl.BlockSpec((tm * pack_rows, LANES), lambda i: (i, 0)), rs(8), rs(8)],
        out_shape=[jax.ShapeDtypeStruct((t, D_MODEL), F32), jax.ShapeDtypeStruct((t * pack_rows, LANES), U32),
                   jax.ShapeDtypeStruct((t, 8), jnp.int32), jax.ShapeDtypeStruct((t, 8), F32)],
        compiler_params=_cp(("arbitrary",)),
        name="merge",
    )(om, oc, osb, h, h, h, x, wbm, wbc, wbs, wo, lg, lb, rwh, rwl, rb)


def _moe_kernel(te_ref, nt_ref, tok_ref, lay_ref, x2_ref, gw_ref, wg_ref, bg_ref, wu_ref, bu_ref, wd_ref, bd_ref,
                out_ref, xs_ref, ys_ref):
    del te_ref, lay_ref
    s = pl.program_id(0)
    pack_rows = D_MODEL // 2 // LANES
    out_rows = D_MODEL // LANES

    @pl.when(s == 0)
    def _():
        out_ref[...] = jnp.zeros_like(out_ref)

    @pl.when(s < nt_ref[0])
    def _():
        base = s * MOE_TILE

        def gather(r, c):
            t = tok_ref[base + r]
            xs_ref[pl.ds(pl.multiple_of(r * pack_rows, pack_rows), pack_rows), :] = (
                x2_ref[pl.ds(pl.multiple_of(t * pack_rows, pack_rows), pack_rows), :])
            return c

        lax.fori_loop(0, MOE_TILE, gather, 0, unroll=8)
        lo, hi = [], []
        for c in range(pack_rows):
            w = xs_ref[pl.ds(c, MOE_TILE, stride=pack_rows), :]
            hi.append(pltpu.bitcast(w << 16, F32).astype(BF16))
            lo.append(pltpu.bitcast(w & jnp.uint32(0xFFFF0000), F32).astype(BF16))
        x = jnp.concatenate(hi + lo, axis=1)
        gt = jnp.minimum(jnp.dot(x, wg_ref[0, 0], preferred_element_type=F32) + bg_ref[0, 0], SWIGLU_LIMIT)
        up = jnp.clip(jnp.dot(x, wu_ref[0, 0], preferred_element_type=F32) + bu_ref[0, 0],
                      -SWIGLU_LIMIT, SWIGLU_LIMIT)
        hid = gt * jax.nn.sigmoid(SWIGLU_ALPHA * gt) * (up + 1.0)
        y = (jnp.dot(hid.astype(BF16), wd_ref[0, 0], preferred_element_type=F32) + bd_ref[0, 0]) * gw_ref[...]
        for c in range(out_rows):
            ys_ref[pl.ds(c, MOE_TILE, stride=out_rows), :] = y[:, c * LANES:(c + 1) * LANES]

        def scatter(r, c):
            t = tok_ref[base + r]
            dst = pl.ds(pl.multiple_of(t * out_rows, out_rows), out_rows)
            out_ref[dst, :] = out_ref[dst, :] + ys_ref[pl.ds(pl.multiple_of(r * out_rows, out_rows), out_rows), :]
            return c

        lax.fori_loop(0, MOE_TILE, scatter, 0, unroll=8)


def _moe(tile_e, n_tiles, tok, layer, x2, gw, wg, bg, wu, bu, wd, bd):
    n_tok = x2.shape[0] // (D_MODEL // 2 // LANES)
    nt = tile_e.shape[0]

    def wspec():
        return pl.BlockSpec((1, 1, D_MODEL, D_MODEL), lambda s, te, n, tk, lay: (lay[0], te[s], 0, 0))

    def bspec():
        return pl.BlockSpec((1, 1, 1, D_MODEL), lambda s, te, n, tk, lay: (lay[0], te[s], 0, 0))

    grid_spec = pltpu.PrefetchScalarGridSpec(
        num_scalar_prefetch=4,
        grid=(nt,),
        in_specs=[pl.BlockSpec(memory_space=pltpu.VMEM),
                  pl.BlockSpec((MOE_TILE, 1), lambda s, te, n, tk, lay: (s, 0)),
                  wspec(), bspec(), wspec(), bspec(), wspec(), bspec()],
        out_specs=pl.BlockSpec(memory_space=pltpu.VMEM),
        scratch_shapes=[pltpu.VMEM((MOE_TILE * (D_MODEL // 2 // LANES), LANES), U32),
                        pltpu.VMEM((MOE_TILE * (D_MODEL // LANES), LANES), F32)])
    return pl.pallas_call(
        _moe_kernel,
        grid_spec=grid_spec,
        out_shape=jax.ShapeDtypeStruct((n_tok * (D_MODEL // LANES), LANES), F32),
        compiler_params=_cp(("arbitrary",)),
        name="moe",
    )(tile_e, n_tiles, tok, layer, x2, gw, wg, bg, wu, bu, wd, bd)


def _route(idx, gw):
    n_tok = idx.shape[0]
    n_assign = n_tok * TOP_K
    n_tiles_max = n_assign // MOE_TILE + N_EXPERTS
    flat_e = idx.reshape(n_assign)
    order = jnp.argsort(flat_e, stable=True).astype(jnp.int32)
    sorted_e = flat_e[order]
    counts = jnp.sum((flat_e[:, None] == jnp.arange(N_EXPERTS, dtype=jnp.int32)[None, :]).astype(jnp.int32), axis=0)
    starts = jnp.cumsum(counts) - counts
    n_tile = (counts + MOE_TILE - 1) // MOE_TILE
    tile_end = jnp.cumsum(n_tile)
    tile_start = tile_end - n_tile
    pos = tile_start[sorted_e] * MOE_TILE + jnp.arange(n_assign, dtype=jnp.int32) - starts[sorted_e]
    tok = jnp.zeros((n_tiles_max * MOE_TILE,), jnp.int32).at[pos].set(order // TOP_K)
    gwp = jnp.zeros((n_tiles_max * MOE_TILE,), F32).at[pos].set(gw.reshape(n_assign)[order])
    total = tile_end[-1]
    slot = jnp.minimum(jnp.arange(n_tiles_max, dtype=jnp.int32), total - 1)
    tile_e = jnp.searchsorted(tile_end, slot, side="right").astype(jnp.int32)
    return tile_e, total.reshape(1).astype(jnp.int32), tok, gwp.reshape(-1, 1)


def _ln2_kernel(f_ref, x_ref, g_ref, b_ref, o_ref, *, alpha):
    tm = x_ref.shape[0]
    rows = D_MODEL // LANES
    ffn = jnp.concatenate([f_ref[pl.ds(c, tm, stride=rows), :] for c in range(rows)], axis=1)
    o_ref[...] = _layer_norm(alpha * x_ref[...] + ffn, g_ref[...], b_ref[...])


def _ln2(ffn2, x1, g, b, *, tm, alpha):
    t = x1.shape[0]
    rows = D_MODEL // LANES
    return pl.pallas_call(
        functools.partial(_ln2_kernel, alpha=alpha),
        grid=(t // tm,),
        in_specs=[pl.BlockSpec((tm * rows, LANES), lambda i: (i, 0)),
                  pl.BlockSpec((tm, D_MODEL), lambda i: (i, 0)),
                  pl.BlockSpec(g.shape, lambda i: (0, 0)), pl.BlockSpec(b.shape, lambda i: (0, 0))],
        out_specs=pl.BlockSpec((tm, D_MODEL), lambda i: (i, 0)),
        out_shape=jax.ShapeDtypeStruct((t, D_MODEL), F32),
        compiler_params=_cp(("arbitrary",)),
        name="ln2",
    )(ffn2, x1, g, b)


def _prep_layer_weights(w_in, q_norm, kv_norm, w_uq, w_uk, w_uv, conv_w, w_bm, w_bc, w_bs, w_o, rw, rb):
    d = w_in.shape[0]

    def z(rows, n):
        return jnp.zeros((rows, n), F32)

    splits = (MLA_QR, MLA_KVR, MLA_DR, CONV_DIM, CONV_DIM, CONV_DIM, SB_H * SB_D, SB_G * SB_D, SB_G * SB_D,
              3 * D_MODEL)
    parts, start = [], 0
    for width in splits:
        parts.append(w_in[:, start:start + width])
        start += width
    cq, ckv, kr, cb, cc, cx, sq, sk, sv, gates = parts
    half = MLA_DR // 2
    sqh = sq.reshape(d, SB_H, SB_D)
    zq = jnp.zeros_like(sqh)
    group0 = (jnp.arange(SB_H) // (SB_H // SB_G) == 0)[None, :, None]
    sq_pad = jnp.where(group0, jnp.concatenate([sqh, zq], -1), jnp.concatenate([zq, sqh], -1))
    sq_pad = sq_pad.reshape(d, SB_H * HEAD_PAD)
    kr_rot = jnp.concatenate([-kr[:, half:], kr[:, :half]], axis=1)
    tail = HEAD_PAD - MLA_DN - MLA_DR
    kra = jnp.concatenate([z(d, MLA_DN), kr, z(d, tail)], axis=1)
    krb = jnp.concatenate([z(d, MLA_DN), kr_rot, z(d, tail)], axis=1)
    w_h = jnp.concatenate([gates, sq_pad, cb, cc, cx, ckv, sk, sv, kra, krb, z(d, C_CQ - C_KRB - LANES), cq],
                          axis=1).astype(BF16)

    qn_, qr_ = w_uq[:, :, :MLA_DN], w_uq[:, :, MLA_DN:]
    qr_rot = jnp.concatenate([-qr_[:, :, half:], qr_[:, :, :half]], axis=-1)
    zn = jnp.zeros((MLA_QR, MLA_H, MLA_DN), F32)
    zt = jnp.zeros((MLA_QR, MLA_H, tail), F32)
    wuq_a = jnp.concatenate([qn_, qr_, zt], axis=-1).reshape(MLA_QR, MLA_H * HEAD_PAD)
    wuq_b = jnp.concatenate([zn, qr_rot, zt], axis=-1).reshape(MLA_QR, MLA_H * HEAD_PAD)
    wuq = jnp.concatenate([wuq_a, wuq_b], axis=1).astype(BF16)
    zk = jnp.zeros((MLA_KVR, MLA_H, HEAD_PAD - MLA_DN), F32)
    wuk = jnp.concatenate([w_uk, zk], axis=-1).reshape(MLA_KVR, MLA_H * HEAD_PAD).astype(BF16)
    wuv = jnp.concatenate([w_uv, zk], axis=-1).reshape(MLA_KVR, MLA_H * HEAD_PAD).astype(BF16)

    eye = jnp.eye(MLA_DR, dtype=F32)
    top = jnp.concatenate([jnp.transpose(w_uk, (1, 2, 0)), jnp.zeros((MLA_H, MLA_DN, LANES), F32)], axis=-1)
    mid = jnp.concatenate([jnp.zeros((MLA_H, MLA_DR, MLA_KVR), F32),
                           jnp.broadcast_to(eye, (MLA_H, MLA_DR, MLA_DR)),
                           jnp.zeros((MLA_H, MLA_DR, LANES - MLA_DR), F32)], axis=-1)
    bot = jnp.zeros((MLA_H, tail, MLA_KVR + LANES), F32)
    w_qext = jnp.concatenate([top, mid, bot], axis=1).astype(BF16)
    w_uvd = jnp.concatenate([jnp.transpose(w_uv, (1, 0, 2)),
                             jnp.zeros((MLA_H, MLA_KVR, HEAD_PAD - MLA_DV), F32)], axis=-1).astype(BF16)

    wbm = jnp.concatenate([w_bm.reshape(MLA_H, MLA_DV, D_MODEL),
                           jnp.zeros((MLA_H, HEAD_PAD - MLA_DV, D_MODEL), F32)], axis=1)
    wbm = wbm.reshape(MLA_H * HEAD_PAD, D_MODEL).astype(BF16)
    wsh = w_bs.reshape(SB_H, SB_D, D_MODEL)
    zs = jnp.zeros_like(wsh)
    wbs = jnp.where(group0.reshape(SB_H, 1, 1), jnp.concatenate([wsh, zs], 1), jnp.concatenate([zs, wsh], 1))
    wbs = wbs.reshape(SB_H * HEAD_PAD, D_MODEL).astype(BF16)
    rwp = jnp.concatenate([rw, z(d, LANES - N_EXPERTS)], axis=1)
    rwh = rwp.astype(BF16)
    rwl = (rwp - rwh.astype(F32)).astype(BF16)
    rbp = jnp.concatenate([rb, jnp.full((LANES - N_EXPERTS,), NEG_BIG, F32)]).reshape(1, LANES)
    cw8 = jnp.concatenate([conv_w, jnp.zeros((8 - conv_w.shape[0], CONV_DIM), F32)], axis=0)
    return dict(w_h=w_h, qn=q_norm.reshape(1, -1), kvn=kv_norm.reshape(1, -1), wuq=wuq, wuk=wuk, wuv=wuv, cw=cw8,
                w_qext=w_qext, w_uvd=w_uvd, wbm=wbm, wbc=w_bc.astype(BF16), wbs=wbs, wo=w_o.astype(BF16),
                rwh=rwh, rwl=rwl, rb=rbp)


def _rope_tables(pos):
    half = MLA_DR // 2
    freq = ROPE_THETA ** (-jnp.arange(half, dtype=F32) / half)
    ang = pos.astype(F32)[:, None] * freq
    n = pos.shape[0]
    ones = jnp.ones((n, MLA_DN), F32)
    tail = HEAD_PAD - MLA_DN - MLA_DR
    cos_t = jnp.concatenate([ones, jnp.cos(ang), jnp.cos(ang), jnp.ones((n, tail), F32)], axis=1)
    sin_t = jnp.concatenate([jnp.zeros((n, MLA_DN), F32), jnp.sin(ang), jnp.sin(ang), jnp.zeros((n, tail), F32)],
                            axis=1)
    return cos_t, sin_t


def _group_tail(xg, h, om, oc, osb, lw, lay, experts, ln1, ln2, *, tm, chunk, alpha):
    x1, x1p, idx8, gw8 = _merge(om, oc, osb, h, xg, lw["wbm"], lw["wbc"], lw["wbs"], lw["wo"], ln1[0], ln1[1],
                                lw["rwh"], lw["rwl"], lw["rb"], tm=tm, alpha=alpha)
    t = xg.shape[0]
    n_chunks = t // chunk
    idx = idx8[:, :TOP_K].reshape(n_chunks, chunk, TOP_K)
    gw = gw8[:, :TOP_K].reshape(n_chunks, chunk, TOP_K)
    tile_e, n_tiles, tok, gwp = jax.vmap(_route)(idx, gw)
    pack_rows = D_MODEL // 2 // LANES
    x1p = x1p.reshape(n_chunks, chunk * pack_rows, LANES)
    wg, bg, wu, bu, wd, bd = experts
    outs = [_moe(tile_e[c], n_tiles[c], tok[c], lay, x1p[c], gwp[c], wg, bg, wu, bu, wd, bd)
            for c in range(n_chunks)]
    ffn2 = outs[0] if n_chunks == 1 else jnp.concatenate(outs, axis=0)
    return _ln2(ffn2, x1, ln2[0], ln2[1], tm=tm, alpha=alpha)


def kernel(x_prompt, x_sample, cache_mla_ckv, cache_mla_krope, cache_sb_k, cache_sb_v, state_conv, page_table, w_in, mla_q_norm, mla_kv_norm, mla_w_uq, mla_w_uk, mla_w_uv, conv_w, w_branch_mla, w_branch_conv, w_branch_sb, w_o, ln1_g, ln1_b, router_w, router_b, exp_w_gate, exp_b_gate, exp_w_up, exp_b_up, exp_w_down, exp_b_down, ln2_g, ln2_b):
    depth = w_in.shape[0]
    batch, seq, d = x_prompt.shape
    nb = x_sample.shape[0]
    n_pages = page_table.shape[1]
    past = n_pages * PAGE
    alpha = (2 * depth) ** 0.25
    tp = batch * seq

    n_pool = cache_sb_k.shape[1]
    c_krt = jnp.transpose(cache_mla_krope, (0, 1, 3, 2))
    c_kt = jnp.transpose(cache_sb_k, (0, 1, 3, 4, 2)).reshape(depth, n_pool, SB_G * SB_D, PAGE)
    c_vt = jnp.transpose(cache_sb_v, (0, 1, 3, 4, 2)).reshape(depth, n_pool, SB_G * SB_D, PAGE)
    pt_flat = page_table.reshape(-1).astype(jnp.int32)

    experts = (exp_w_gate.astype(BF16), exp_b_gate.reshape(depth, N_EXPERTS, 1, d),
               exp_w_up.astype(BF16), exp_b_up.reshape(depth, N_EXPERTS, 1, d),
               exp_w_down.astype(BF16), exp_b_down.reshape(depth, N_EXPERTS, 1, d))

    cos_p, sin_p = _rope_tables(jnp.arange(seq))
    cos_s, sin_s = _rope_tables(jnp.full((nb,), past))

    xp = x_prompt.reshape(tp, d)
    xs = x_sample.reshape(nb, d)
    new_p = [[], [], [], [], []]
    new_s = [[], [], [], [], []]
    for l in range(depth):
        lay = jnp.full((1,), l, jnp.int32)
        lw = _prep_layer_weights(w_in[l], mla_q_norm[l], mla_kv_norm[l], mla_w_uq[l], mla_w_uk[l], mla_w_uv[l],
                                 conv_w[l], w_branch_mla[l], w_branch_conv[l], w_branch_sb[l], w_o[l],
                                 router_w[l], router_b[l])
        ln1 = (ln1_g[l].reshape(1, d), ln1_b[l].reshape(1, d))
        ln2 = (ln2_g[l].reshape(1, d), ln2_b[l].reshape(1, d))

        h = _linear(xp, lw["w_h"], F32, 1024, 768)
        ckv, kr, q, k, v, oc, tail, sq, sk, sv = _prep(
            h, cos_p, sin_p, lw["qn"], lw["kvn"], lw["wuq"], lw["wuk"], lw["wuv"], lw["cw"], None,
            tm=512, decode=False, tiles_per_seq=seq // 512)
        om = _mla_attn(q, k, v, batch=batch, seq=seq, tq=512)
        osb = _sb_attn(sq, sk, sv, batch=batch, seq=seq, tq=256)
        new_p[0].append(ckv.reshape(batch, seq, MLA_KVR))
        new_p[1].append(kr.reshape(batch, seq, MLA_DR))
        new_p[2].append(h[:, C_SBK:C_SBK + LANES].reshape(batch, seq, SB_G, SB_D))
        new_p[3].append(h[:, C_SBV:C_SBV + LANES].reshape(batch, seq, SB_G, SB_D))
        new_p[4].append(tail.reshape(batch, seq // 512, 8, CONV_DIM)[:, -1, 6:, :])
        xp = _group_tail(xp, h, om, oc, osb, lw, lay, experts, ln1, ln2, tm=512, chunk=seq, alpha=alpha)

        hs = _linear(xs, lw["w_h"], F32, nb, 768)
        st = state_conv[l]
        ckv_s, kr_s, q_s, _, _, oc_s, u_s, sq_s, _, _ = _prep(
            hs, cos_s, sin_s, lw["qn"], lw["kvn"], lw["wuq"], lw["wuk"], lw["wuv"], lw["cw"],
            (st[:, 0, :], st[:, 1, :]), tm=nb, decode=True, tiles_per_seq=1)
        qe = _headwise_linear(q_s, lw["w_qext"], BF16).reshape(nb, MLA_H, MLA_KVR + LANES)
        olat, osb_s = _decode_attn(pt_flat, lay, qe, sq_s.reshape(nb, SB_H, HEAD_PAD),
                                   ckv_s.reshape(nb, 1, MLA_KVR), kr_s.reshape(nb, 1, MLA_DR),
                                   cache_mla_ckv, c_krt, c_kt, c_vt, pages_per_step=8)
        om_s = _headwise_linear(olat.reshape(nb, MLA_H * MLA_KVR), lw["w_uvd"], BF16)
        new_s[0].append(ckv_s.reshape(nb, 1, MLA_KVR))
        new_s[1].append(kr_s.reshape(nb, 1, MLA_DR))
        new_s[2].append(hs[:, C_SBK:C_SBK + LANES].reshape(nb, 1, SB_G, SB_D))
        new_s[3].append(hs[:, C_SBV:C_SBV + LANES].reshape(nb, 1, SB_G, SB_D))
        new_s[4].append(jnp.stack([st[:, 1, :], u_s], axis=1))
        xs = _group_tail(xs, hs, om_s, oc_s, osb_s.reshape(nb, SB_H * HEAD_PAD), lw, lay, experts, ln1, ln2,
                         tm=nb, chunk=nb, alpha=alpha)

    outs_p = [jnp.stack(a) for a in new_p]
    outs_s = [jnp.stack(a) for a in new_s]
    return (xp.reshape(batch, seq, d), xs.reshape(nb, 1, d), *outs_p, *outs_s)
```

```python
import functools

import jax
import jax.numpy as jnp
from jax import lax
from jax.experimental import pallas as pl
from jax.experimental.pallas import tpu as pltpu

F32 = jnp.float32
BF16 = jnp.bfloat16
U32 = jnp.uint32

D_MODEL = 1024
PAGE = 128
MLA_H = 8
MLA_DN = 64
MLA_DR = 32
MLA_DV = 64
MLA_QR = 384
MLA_KVR = 256
MLA_SCALE = (MLA_DN + MLA_DR) ** -0.5
ROPE_THETA = 10000.0
CONV_DIM = 512
SB_H = 8
SB_G = 2
SB_D = 64
SB_SCALE = SB_D ** -0.5
N_EXPERTS = 32
TOP_K = 4
SWIGLU_LIMIT = 7.0
SWIGLU_ALPHA = 1.702
LN_EPS = 1e-5
RMS_EPS = 1e-6
LANES = 128
HEAD_PAD = 128
MOE_TILE = 256
NEG_BIG = -1e30
SB_DEAD_LOG = -120.0

C_GATE, C_SBQ, C_CB, C_CC, C_CX, C_CKV, C_SBK, C_SBV, C_KRA, C_KRB, C_CQ = (
    0, 3072, 4096, 4608, 5120, 5632, 5888, 6016, 6144, 6272, 6528)
H_COLS = 6912
VMEM_LIMIT = 56 * 1024 * 1024


def _cp(sem, vmem=VMEM_LIMIT):
    return pltpu.CompilerParams(dimension_semantics=sem, vmem_limit_bytes=vmem)


def _linear_kernel(x_ref, w_ref, o_ref, xb_ref):
    @pl.when(pl.program_id(1) == 0)
    def _():
        xb_ref[...] = x_ref[...].astype(BF16)

    o_ref[...] = jnp.dot(xb_ref[...], w_ref[...], preferred_element_type=F32).astype(o_ref.dtype)


def _linear(x, w, out_dtype, tm, tn):
    m, k = x.shape
    n = w.shape[1]
    return pl.pallas_call(
        _linear_kernel,
        grid=(m // tm, n // tn),
        in_specs=[pl.BlockSpec((tm, k), lambda i, j: (i, 0)),
                  pl.BlockSpec((k, tn), lambda i, j: (0, j))],
        out_specs=pl.BlockSpec((tm, tn), lambda i, j: (i, j)),
        out_shape=jax.ShapeDtypeStruct((m, n), out_dtype),
        scratch_shapes=[pltpu.VMEM((tm, k), BF16)],
        compiler_params=_cp(("arbitrary", "arbitrary")),
        name="linear",
    )(x, w)


def _headwise_kernel(x_ref, w_ref, o_ref):
    o_ref[...] = jnp.dot(x_ref[...], w_ref[0], preferred_element_type=F32).astype(o_ref.dtype)


def _headwise_linear(x, w, out_dtype):
    m = x.shape[0]
    h, k, n = w.shape
    return pl.pallas_call(
        _headwise_kernel,
        grid=(h,),
        in_specs=[pl.BlockSpec((m, k), lambda i: (0, i)),
                  pl.BlockSpec((1, k, n), lambda i: (i, 0, 0))],
        out_specs=pl.BlockSpec((m, n), lambda i: (0, i)),
        out_shape=jax.ShapeDtypeStruct((m, h * n), out_dtype),
        compiler_params=_cp(("arbitrary",)),
        name="headwise_linear",
    )(x, w)


def _rms(x, g):
    return x * lax.rsqrt(jnp.mean(x * x, axis=-1, keepdims=True) + RMS_EPS) * g


def _prep_kernel(cq_ref, ckv_ref, kra_ref, krb_ref, cb_ref, cc_ref, cx_ref, pa_ref, pb_ref,
                 sq_ref, sk_ref, sv_ref, cos_ref, sin_ref, qn_ref, kvn_ref, wuq_ref, wuk_ref, wuv_ref, cw_ref,
                 ckv_o, kr_o, q_o, k_o, v_o, oc_o, tail_o, sq_o, sk_o, sv_o, *scratch, decode, tiles_per_seq):
    tm = cq_ref.shape[0]
    cos = cos_ref[...]
    sin = sin_ref[...]
    cos8 = jnp.concatenate([cos] * MLA_H, axis=1)
    sin8 = jnp.concatenate([sin] * MLA_H, axis=1)

    cqn = _rms(cq_ref[...], qn_ref[...]).astype(BF16)
    qq = jnp.dot(cqn, wuq_ref[...], preferred_element_type=F32)
    width = MLA_H * HEAD_PAD
    q_o[...] = (qq[:, :width] * cos8 + qq[:, width:] * sin8).astype(BF16)

    ckv = _rms(ckv_ref[...], kvn_ref[...])
    ckv_o[...] = ckv
    ckvb = ckv.astype(BF16)
    krp = kra_ref[...] * cos + krb_ref[...] * sin
    kr_o[...] = krp[:, MLA_DN:MLA_DN + MLA_DR]
    kn = jnp.dot(ckvb, wuk_ref[...], preferred_element_type=F32)
    k_o[...] = (kn + jnp.concatenate([krp] * MLA_H, axis=1)).astype(BF16)
    v_o[...] = jnp.dot(ckvb, wuv_ref[...], preferred_element_type=F32).astype(BF16)

    u = cc_ref[...] * cx_ref[...]
    w0 = cw_ref[0:1, :]
    w1 = cw_ref[1:2, :]
    w2 = cw_ref[2:3, :]
    if decode:
        conv = w0 * pa_ref[...] + w1 * pb_ref[...] + w2 * u
        tail_o[...] = u
    else:
        ext_ref, = scratch
        first = (pl.program_id(0) % tiles_per_seq) == 0
        prev = pa_ref[...] * pb_ref[...]
        ext_ref[0:8, :] = jnp.where(first, 0.0, prev)
        ext_ref[8:, :] = u
        conv = w0 * ext_ref[pl.ds(6, tm), :] + w1 * ext_ref[pl.ds(7, tm), :] + w2 * u
        tail_o[...] = u[tm - 8:, :]
    oc_o[...] = (cb_ref[...] * conv).astype(BF16)

    sq_o[...] = sq_ref[...].astype(BF16)
    sk_o[...] = sk_ref[...].astype(BF16)
    sv_o[...] = sv_ref[...].astype(BF16)


def _prep(h, cos_t, sin_t, qn, kvn, wuq, wuk, wuv, cw, state, *, tm, decode, tiles_per_seq):
    t = h.shape[0]
    nt = t // tm
    n_tab = cos_t.shape[0] // tm

    def hs(width, col):
        return pl.BlockSpec((tm, width), lambda i: (i, col // width))

    def full(a):
        return pl.BlockSpec(a.shape, lambda i: (0,) * a.ndim)

    if decode:
        pa, pb = state
        pa_spec = pb_spec = pl.BlockSpec((tm, CONV_DIM), lambda i: (i, 0))
        tail_shape, tail_spec = (t, CONV_DIM), pl.BlockSpec((tm, CONV_DIM), lambda i: (i, 0))
        scratch = []
    else:
        pa = pb = h
        r8 = tm // 8
        pa_spec = pl.BlockSpec((8, CONV_DIM), lambda i: (jnp.maximum(i * r8 - 1, 0), C_CC // CONV_DIM))
        pb_spec = pl.BlockSpec((8, CONV_DIM), lambda i: (jnp.maximum(i * r8 - 1, 0), C_CX // CONV_DIM))
        tail_shape, tail_spec = (nt * 8, CONV_DIM), pl.BlockSpec((8, CONV_DIM), lambda i: (i, 0))
        scratch = [pltpu.VMEM((tm + 8, CONV_DIM), F32)]
    tab_spec = pl.BlockSpec((tm, LANES), lambda i: (i % n_tab, 0))
    wide = MLA_H * HEAD_PAD
    out_shape = [
        jax.ShapeDtypeStruct((t, MLA_KVR), F32), jax.ShapeDtypeStruct((t, MLA_DR), F32),
        jax.ShapeDtypeStruct((t, wide), BF16), jax.ShapeDtypeStruct((t, wide), BF16),
        jax.ShapeDtypeStruct((t, wide), BF16), jax.ShapeDtypeStruct((t, CONV_DIM), BF16),
        jax.ShapeDtypeStruct(tail_shape, F32), jax.ShapeDtypeStruct((t, SB_H * HEAD_PAD), BF16),
        jax.ShapeDtypeStruct((t, LANES), BF16), jax.ShapeDtypeStruct((t, LANES), BF16)]

    def os(width):
        return pl.BlockSpec((tm, width), lambda i: (i, 0))

    out_specs = [os(MLA_KVR), os(MLA_DR), os(wide), os(wide), os(wide), os(CONV_DIM), tail_spec,
                 os(SB_H * HEAD_PAD), os(LANES), os(LANES)]
    return pl.pallas_call(
        functools.partial(_prep_kernel, decode=decode, tiles_per_seq=tiles_per_seq),
        grid=(nt,),
        in_specs=[hs(MLA_QR, C_CQ), hs(MLA_KVR, C_CKV), hs(LANES, C_KRA), hs(LANES, C_KRB),
                  hs(CONV_DIM, C_CB), hs(CONV_DIM, C_CC), hs(CONV_DIM, C_CX), pa_spec, pb_spec,
                  hs(SB_H * HEAD_PAD, C_SBQ), hs(LANES, C_SBK), hs(LANES, C_SBV),
                  tab_spec, tab_spec, full(qn), full(kvn), full(wuq), full(wuk), full(wuv), full(cw)],
        out_specs=out_specs,
        out_shape=out_shape,
        scratch_shapes=scratch,
        compiler_params=_cp(("arbitrary",)),
        name="prep_decode" if decode else "prep",
    )(h, h, h, h, h, h, h, pa, pb, h, h, h, cos_t, sin_t, qn, kvn, wuq, wuk, wuv, cw)


def _mla_attn_kernel(q_ref, k_ref, v_ref, o_ref, *, tq):
    qi = pl.program_id(2)
    q = q_ref[...]

    def scores(j):
        off = pl.multiple_of(j * tq, tq)
        k = k_ref[pl.ds(off, tq), :]
        v = v_ref[pl.ds(off, tq), :]
        s = lax.dot_general(q, k, (((1,), (1,)), ((), ())), preferred_element_type=F32) * MLA_SCALE
        return s, v

    def update(carry, s, v):
        m, l, acc = carry
        m_new = jnp.maximum(m, jnp.max(s, axis=-1, keepdims=True))
        alpha = jnp.exp(m - m_new)
        p = jnp.exp(s - m_new)
        l = alpha * l + jnp.sum(p, axis=-1, keepdims=True)
        acc = alpha * acc + jnp.dot(p.astype(BF16), v, preferred_element_type=F32)
        return m_new, l, acc

    def body(j, carry):
        s, v = scores(j)
        return update(carry, s, v)

    init = (jnp.full((tq, 1), NEG_BIG, F32), jnp.zeros((tq, 1), F32), jnp.zeros((tq, HEAD_PAD), F32))
    carry = lax.fori_loop(0, qi, body, init)
    s, v = scores(qi)
    row = lax.broadcasted_iota(jnp.int32, (tq, tq), 0)
    col = lax.broadcasted_iota(jnp.int32, (tq, tq), 1)
    s = jnp.where(col <= row, s, NEG_BIG)
    _, l, acc = update(carry, s, v)
    o_ref[...] = (acc / l).astype(o_ref.dtype)


def _mla_attn(q, k, v, *, batch, seq, tq):
    nq = seq // tq
    qspec = pl.BlockSpec((tq, HEAD_PAD), lambda b, h, i: (b * nq + i, h))
    kspec = pl.BlockSpec((seq, HEAD_PAD), lambda b, h, i: (b, h))
    return pl.pallas_call(
        functools.partial(_mla_attn_kernel, tq=tq),
        grid=(batch, MLA_H, nq),
        in_specs=[qspec, kspec, kspec],
        out_specs=qspec,
        out_shape=jax.ShapeDtypeStruct(q.shape, BF16),
        compiler_params=_cp(("arbitrary", "arbitrary", "arbitrary")),
        name="mla_attn",
    )(q, k, v)


def _log_sigmoid(z):
    return jnp.minimum(z, 0.0) - jnp.log1p(jnp.exp(-jnp.abs(z)))


def _strict_suffix(lk, upper):
    hi = lk.astype(BF16)
    lo = (lk - hi.astype(F32)).astype(BF16)
    return (jnp.dot(hi, upper, preferred_element_type=F32) + jnp.dot(lo, upper, preferred_element_type=F32))


def _upper(n):
    j = lax.broadcasted_iota(jnp.int32, (n, n), 0)
    s = lax.broadcasted_iota(jnp.int32, (n, n), 1)
    return jnp.where(j > s, 1.0, 0.0).astype(BF16)


def _sb_attn_kernel(q_ref, k_ref, v_ref, o_ref, *, tq):
    qi = pl.program_id(2)
    q = q_ref[...]
    upper = _upper(tq)

    def block(j, carry, acc, diag):
        off = pl.multiple_of(j * tq, tq)
        k = k_ref[pl.ds(off, tq), :]
        v = v_ref[pl.ds(off, tq), :]
        z = lax.dot_general(q, k, (((1,), (1,)), ((), ())), preferred_element_type=F32) * SB_SCALE
        lb = _log_sigmoid(z)
        lk = lb - z
        if diag:
            row = lax.broadcasted_iota(jnp.int32, (tq, tq), 0)
            col = lax.broadcasted_iota(jnp.int32, (tq, tq), 1)
            valid = col < row
            lk = jnp.where(valid, lk, 0.0)
        a = jnp.exp(lb + _strict_suffix(lk, upper) + carry)
        if diag:
            a = jnp.where(valid, a, 0.0)
        acc = acc + jnp.dot(a.astype(BF16), v, preferred_element_type=F32)
        carry = carry + jnp.sum(lk, axis=-1, keepdims=True)
        return carry, acc

    carry, acc = block(qi, jnp.zeros((tq, 1), F32), jnp.zeros((tq, HEAD_PAD), F32), True)

    def cond(st):
        return jnp.logical_and(st[0] < qi, jnp.max(st[1]) > SB_DEAD_LOG)

    def body(st):
        t, carry, acc = st
        carry, acc = block(qi - 1 - t, carry, acc, False)
        return t + 1, carry, acc

    _, carry, acc = lax.while_loop(cond, body, (jnp.int32(0), carry, acc))
    o_ref[...] = acc.astype(o_ref.dtype)


def _sb_attn(q, k, v, *, batch, seq, tq):
    nq = seq // tq
    qspec = pl.BlockSpec((tq, HEAD_PAD), lambda b, h, i: (b * nq + i, h))
    kspec = pl.BlockSpec((seq, LANES), lambda b, h, i: (b, 0))
    return pl.pallas_call(
        functools.partial(_sb_attn_kernel, tq=tq),
        grid=(batch, SB_H, nq),
        in_specs=[qspec, kspec, kspec],
        out_specs=qspec,
        out_shape=jax.ShapeDtypeStruct(q.shape, BF16),
        compiler_params=_cp(("arbitrary", "arbitrary", "arbitrary")),
        name="sb_attn",
    )(q, k, v)


def _decode_kernel(pt_ref, lay_ref, qe_ref, qs_ref, cn_ref, kn_ref, *refs, pages_per_step, n_steps):
    del pt_ref, lay_ref
    p_ = pages_per_step
    caches = refs[:4 * p_]
    olat_ref, osb_ref, m_ref, l_ref, acc_ref, car_ref, accs_ref = refs[4 * p_:]
    j = pl.program_id(1)
    qe = qe_ref[0]
    ql = qe[:, :MLA_KVR]
    qr = qe[:, MLA_KVR:MLA_KVR + MLA_DR]
    qs = qs_ref[0]

    @pl.when(j == 0)
    def _():
        cn = cn_ref[0]
        kn = kn_ref[0]
        s0 = (jnp.sum(ql.astype(F32) * cn, axis=-1, keepdims=True)
              + jnp.sum(qr.astype(F32) * kn, axis=-1, keepdims=True)) * MLA_SCALE
        m_ref[...] = s0
        l_ref[...] = jnp.ones_like(l_ref)
        acc_ref[...] = jnp.broadcast_to(cn, acc_ref.shape)
        car_ref[...] = jnp.zeros_like(car_ref)
        accs_ref[...] = jnp.zeros_like(accs_ref)

    upper = _upper(PAGE)
    contract_last = (((1,), (1,)), ((), ()))
    ckv = jnp.concatenate([caches[i][0, 0].astype(BF16) for i in range(p_)], axis=0)
    kr_t = jnp.concatenate([caches[p_ + i][0, 0].astype(BF16) for i in range(p_)], axis=1)
    s = (lax.dot_general(ql, ckv, contract_last, preferred_element_type=F32)
         + jnp.dot(qr, kr_t, preferred_element_type=F32)) * MLA_SCALE
    m = m_ref[...]
    m_new = jnp.maximum(m, jnp.max(s, axis=-1, keepdims=True))
    alpha = jnp.exp(m - m_new)
    p = jnp.exp(s - m_new)
    l_ref[...] = alpha * l_ref[...] + jnp.sum(p, axis=-1, keepdims=True)
    acc_ref[...] = alpha * acc_ref[...] + jnp.dot(p.astype(BF16), ckv, preferred_element_type=F32)
    m_ref[...] = m_new

    @pl.when(jnp.max(car_ref[...]) > SB_DEAD_LOG)
    def _():
        for i in range(p_):
            @pl.when(jnp.max(car_ref[...]) > SB_DEAD_LOG)
            def _(i=i):
                k_t = caches[2 * p_ + i][0, 0].astype(BF16)
                v_t = caches[3 * p_ + i][0, 0].astype(BF16)
                z = jnp.dot(qs, k_t, preferred_element_type=F32) * SB_SCALE
                lb = _log_sigmoid(z)
                lk = lb - z
                a = jnp.exp(lb + _strict_suffix(lk, upper) + car_ref[...])
                accs_ref[...] += lax.dot_general(a.astype(BF16), v_t, contract_last, preferred_element_type=F32)
                car_ref[...] += jnp.sum(lk, axis=-1, keepdims=True)

    @pl.when(j == n_steps - 1)
    def _():
        olat_ref[0] = (acc_ref[...] / l_ref[...]).astype(olat_ref.dtype)
        osb_ref[0] = accs_ref[...].astype(osb_ref.dtype)


def _decode_attn(page_table, layer, qe, qs, ckv_new, kr_new, c_ckv, c_krt, c_kt, c_vt, *, pages_per_step):
    nb = qe.shape[0]
    n_pages = page_table.shape[0] // nb
    p_ = pages_per_step
    n_steps = n_pages // p_

    def cache_spec(shape, i):
        def imap(b, j, pt, lay):
            return (lay[0], pt[b * n_pages + (n_pages - 1 - (j * p_ + i))], 0, 0)
        return pl.BlockSpec((1, 1) + shape, imap)

    def seq_spec(shape):
        return pl.BlockSpec((1,) + shape, lambda b, j, pt, lay: (b, 0, 0))

    cache_specs, cache_args = [], []
    for arr in (c_ckv, c_krt, c_kt, c_vt):
        for i in range(p_):
            cache_specs.append(cache_spec(arr.shape[2:], i))
            cache_args.append(arr)
    grid_spec = pltpu.PrefetchScalarGridSpec(
        num_scalar_prefetch=2,
        grid=(nb, n_steps),
        in_specs=[seq_spec(qe.shape[1:]), seq_spec(qs.shape[1:]), seq_spec((1, MLA_KVR)), seq_spec((1, MLA_DR))]
        + cache_specs,
        out_specs=[seq_spec((MLA_H, MLA_KVR)), seq_spec((SB_H, HEAD_PAD))],
        scratch_shapes=[pltpu.VMEM((MLA_H, 1), F32), pltpu.VMEM((MLA_H, 1), F32), pltpu.VMEM((MLA_H, MLA_KVR), F32),
                        pltpu.VMEM((SB_H, 1), F32), pltpu.VMEM((SB_H, HEAD_PAD), F32)])
    return pl.pallas_call(
        functools.partial(_decode_kernel, pages_per_step=p_, n_steps=n_steps),
        grid_spec=grid_spec,
        out_shape=[jax.ShapeDtypeStruct((nb, MLA_H, MLA_KVR), BF16), jax.ShapeDtypeStruct((nb, SB_H, HEAD_PAD), BF16)],
        compiler_params=_cp(("arbitrary", "arbitrary")),
        name="decode_attn",
    )(page_table, layer, qe, qs, ckv_new, kr_new, *cache_args)


def _layer_norm(x, g, b):
    mu = jnp.mean(x, axis=-1, keepdims=True)
    xc = x - mu
    var = jnp.mean(xc * xc, axis=-1, keepdims=True)
    return xc * lax.rsqrt(var + LN_EPS) * g + b


def _merge_kernel(om_ref, oc_ref, os_ref, g0_ref, g1_ref, g2_ref, x_ref, wbm_ref, wbc_ref, wbs_ref, wo_ref,
                  lg_ref, lb_ref, rwh_ref, rwl_ref, rb_ref, x1_ref, xp_ref, idx_ref, gw_ref, *, alpha):
    tm = x_ref.shape[0]
    merged = (jax.nn.sigmoid(g0_ref[...]) * jnp.dot(om_ref[...], wbm_ref[...], preferred_element_type=F32)
              + jax.nn.sigmoid(g1_ref[...]) * jnp.dot(oc_ref[...], wbc_ref[...], preferred_element_type=F32)
              + jax.nn.sigmoid(g2_ref[...]) * jnp.dot(os_ref[...], wbs_ref[...], preferred_element_type=F32))
    mix = jnp.dot(merged.astype(BF16), wo_ref[...], preferred_element_type=F32)
    x1 = _layer_norm(alpha * x_ref[...] + mix, lg_ref[...], lb_ref[...])
    x1_ref[...] = x1
    xb = x1.astype(BF16)

    bits = pltpu.bitcast(xb.astype(F32), U32)
    half = D_MODEL // 2
    word = bits[:, half:] | (bits[:, :half] >> 16)
    for c in range(half // LANES):
        xp_ref[pl.ds(c, tm, stride=half // LANES), :] = word[:, c * LANES:(c + 1) * LANES]

    xl = (x1 - xb.astype(F32)).astype(BF16)
    logits = (jnp.dot(xb, rwh_ref[...], preferred_element_type=F32) + jnp.dot(xl, rwh_ref[...], preferred_element_type=F32)
              + jnp.dot(xb, rwl_ref[...], preferred_element_type=F32) + rb_ref[...])
    lane = lax.broadcasted_iota(jnp.int32, logits.shape, 1)
    vals, idxs = [], []
    for _ in range(TOP_K):
        mx = jnp.max(logits, axis=-1, keepdims=True)
        ix = jnp.min(jnp.where(logits == mx, lane, LANES), axis=-1, keepdims=True)
        vals.append(mx)
        idxs.append(ix)
        logits = jnp.where(lane == ix, NEG_BIG * 2, logits)
    es = [jnp.exp(v - vals[0]) for v in vals]
    den = es[0] + es[1] + es[2] + es[3]
    lane8 = lax.broadcasted_iota(jnp.int32, (tm, 8), 1)
    idx_out = jnp.zeros((tm, 8), jnp.int32)
    gw_out = jnp.zeros((tm, 8), F32)
    for kk in range(TOP_K):
        idx_out = jnp.where(lane8 == kk, idxs[kk], idx_out)
        gw_out = jnp.where(lane8 == kk, es[kk] / den, gw_out)
    idx_ref[...] = idx_out
    gw_ref[...] = gw_out


def _merge(om, oc, osb, h, x, wbm, wbc, wbs, wo, lg, lb, rwh, rwl, rb, *, tm, alpha):
    t = x.shape[0]

    def rs(width, col=0):
        return pl.BlockSpec((tm, width), lambda i: (i, col))

    def full(a):
        return pl.BlockSpec(a.shape, lambda i: (0,) * a.ndim)

    pack_rows = D_MODEL // 2 // LANES
    return pl.pallas_call(
        functools.partial(_merge_kernel, alpha=alpha),
        grid=(t // tm,),
        in_specs=[rs(om.shape[1]), rs(CONV_DIM), rs(osb.shape[1]), rs(D_MODEL, 0), rs(D_MODEL, 1), rs(D_MODEL, 2),
                  rs(D_MODEL), full(wbm), full(wbc), full(wbs), full(wo), full(lg), full(lb), full(rwh), full(rwl), full(rb)],
        out_specs=[rs(D_MODEL), pl.BlockSpec((tm * pack_rows, LANES), lambda i: (i, 0)), rs(8), rs(8)],
        out_shape=[jax.ShapeDtypeStruct((t, D_MODEL), F32), jax.ShapeDtypeStruct((t * pack_rows, LANES), U32),
                   jax.ShapeDtypeStruct((t, 8), jnp.int32), jax.ShapeDtypeStruct((t, 8), F32)],
        compiler_params=_cp(("arbitrary",)),
        name="merge",
    )(om, oc, osb, h, h, h, x, wbm, wbc, wbs, wo, lg, lb, rwh, rwl, rb)


def _moe_kernel(te_ref, nt_ref, src_ref, nv_ref, tok_ref, lay_ref, gws_ref, x2_ref,
                wg_ref, bg_ref, wu_ref, bu_ref, wd_ref, bd_ref, out_ref, xs_ref, ys_ref):
    del te_ref, lay_ref
    s = pl.program_id(0)
    pack_rows = D_MODEL // 2 // LANES
    out_rows = D_MODEL // LANES
    n_assign = tok_ref.shape[0]
    n_tok = x2_ref.shape[0] // pack_rows
    batch = 8

    @pl.when(s == 0)
    def _():
        out_ref[...] = jnp.zeros_like(out_ref)

    @pl.when(s < nt_ref[0])
    def _():
        src = src_ref[s]
        nv = nv_ref[s]

        def gather(r, c):
            t = jnp.where(r < nv, tok_ref[jnp.minimum(src + r, n_assign - 1)], 0)
            xs_ref[pl.ds(pl.multiple_of(r * pack_rows, pack_rows), pack_rows), :] = (
                x2_ref[pl.ds(pl.multiple_of(t * pack_rows, pack_rows), pack_rows), :])
            return c

        lax.fori_loop(0, MOE_TILE, gather, 0, unroll=8)
        lo, hi = [], []
        for c in range(pack_rows):
            w = xs_ref[pl.ds(c, MOE_TILE, stride=pack_rows), :]
            hi.append(pltpu.bitcast(w << 16, F32).astype(BF16))
            lo.append(pltpu.bitcast(w & jnp.uint32(0xFFFF0000), F32).astype(BF16))
        x = jnp.concatenate(hi + lo, axis=1)
        gt = jnp.minimum(jnp.dot(x, wg_ref[0, 0], preferred_element_type=F32) + bg_ref[0, 0], SWIGLU_LIMIT)
        up = jnp.clip(jnp.dot(x, wu_ref[0, 0], preferred_element_type=F32) + bu_ref[0, 0],
                      -SWIGLU_LIMIT, SWIGLU_LIMIT)
        hid = gt * jax.nn.sigmoid(SWIGLU_ALPHA * gt) * (up + 1.0)
        y = jnp.dot(hid.astype(BF16), wd_ref[0, 0], preferred_element_type=F32) + bd_ref[0, 0]
        for c in range(out_rows):
            ys_ref[pl.ds(c, MOE_TILE, stride=out_rows), :] = y[:, c * LANES:(c + 1) * LANES]

        def scatter(g, c):
            pending = []
            for i in range(batch):
                r = g * batch + i
                j = jnp.minimum(src + r, n_assign - 1)
                t = jnp.where(r < nv, tok_ref[j], n_tok)
                dst = pl.ds(pl.multiple_of(t * out_rows, out_rows), out_rows)
                row = ys_ref[pl.ds(pl.multiple_of(r * out_rows, out_rows), out_rows), :]
                pending.append((dst, out_ref[dst, :] + row * gws_ref[j]))
            for dst, val in pending:
                out_ref[dst, :] = val
            return c

        lax.fori_loop(0, MOE_TILE // batch, scatter, 0, unroll=2)


def _moe(tile_e, n_tiles, src, nval, tok, layer, gws, x2, wg, bg, wu, bu, wd, bd):
    n_tok = x2.shape[0] // (D_MODEL // 2 // LANES)
    nt = tile_e.shape[0]

    def wspec():
        return pl.BlockSpec((1, 1, D_MODEL, D_MODEL), lambda s, te, n, sr, nv, tk, lay: (lay[0], te[s], 0, 0))

    def bspec():
        return pl.BlockSpec((1, 1, 1, D_MODEL), lambda s, te, n, sr, nv, tk, lay: (lay[0], te[s], 0, 0))

    grid_spec = pltpu.PrefetchScalarGridSpec(
        num_scalar_prefetch=6,
        grid=(nt,),
        in_specs=[pl.BlockSpec(memory_space=pltpu.SMEM), pl.BlockSpec(memory_space=pltpu.VMEM),
                  wspec(), bspec(), wspec(), bspec(), wspec(), bspec()],
        out_specs=pl.BlockSpec(memory_space=pltpu.VMEM),
        scratch_shapes=[pltpu.VMEM((MOE_TILE * (D_MODEL // 2 // LANES), LANES), U32),
                        pltpu.VMEM((MOE_TILE * (D_MODEL // LANES), LANES), F32)])
    return pl.pallas_call(
        _moe_kernel,
        grid_spec=grid_spec,
        out_shape=jax.ShapeDtypeStruct(((n_tok + 1) * (D_MODEL // LANES), LANES), F32),
        compiler_params=_cp(("arbitrary",)),
        name="moe",
    )(tile_e, n_tiles, src, nval, tok, layer, gws, x2, wg, bg, wu, bu, wd, bd)


def _route(idx, gw):
    n_tok = idx.shape[0]
    n_assign = n_tok * TOP_K
    n_tiles_max = n_assign // MOE_TILE + N_EXPERTS
    flat_e = idx.reshape(n_assign)
    _, order, gws = lax.sort((flat_e, jnp.arange(n_assign, dtype=jnp.int32), gw.reshape(n_assign)),
                             num_keys=1, is_stable=True)
    tok = order // TOP_K
    experts = jnp.arange(N_EXPERTS, dtype=jnp.int32)
    counts = jnp.sum((flat_e[:, None] == experts[None, :]).astype(jnp.int32), axis=0)
    starts = jnp.cumsum(counts) - counts
    n_tile = (counts + MOE_TILE - 1) // MOE_TILE
    tile_end = jnp.cumsum(n_tile)
    tile_start = tile_end - n_tile
    total = tile_end[-1]
    slot = jnp.minimum(jnp.arange(n_tiles_max, dtype=jnp.int32), total - 1)
    tile_e = jnp.sum((tile_end[None, :] <= slot[:, None]).astype(jnp.int32), axis=1)
    onehot = tile_e[:, None] == experts[None, :]

    def pick(v):
        return jnp.sum(jnp.where(onehot, v[None, :], 0), axis=1)

    k = slot - pick(tile_start)
    src = pick(starts) + k * MOE_TILE
    nval = jnp.clip(pick(counts) - k * MOE_TILE, 0, MOE_TILE)
    return tile_e, total.reshape(1), src, nval, tok, gws


def _ln2_kernel(f_ref, x_ref, g_ref, b_ref, o_ref, *, alpha):
    tm = x_ref.shape[0]
    rows = D_MODEL // LANES
    ffn = jnp.concatenate([f_ref[pl.ds(c, tm, stride=rows), :] for c in range(rows)], axis=1)
    o_ref[...] = _layer_norm(alpha * x_ref[...] + ffn, g_ref[...], b_ref[...])


def _ln2(ffn2, x1, g, b, *, tm, alpha):
    t = x1.shape[0]
    rows = D_MODEL // LANES
    return pl.pallas_call(
        functools.partial(_ln2_kernel, alpha=alpha),
        grid=(t // tm,),
        in_specs=[pl.BlockSpec((tm * rows, LANES), lambda i: (i, 0)),
                  pl.BlockSpec((tm, D_MODEL), lambda i: (i, 0)),
                  pl.BlockSpec(g.shape, lambda i: (0, 0)), pl.BlockSpec(b.shape, lambda i: (0, 0))],
        out_specs=pl.BlockSpec((tm, D_MODEL), lambda i: (i, 0)),
        out_shape=jax.ShapeDtypeStruct((t, D_MODEL), F32),
        compiler_params=_cp(("arbitrary",)),
        name="ln2",
    )(ffn2, x1, g, b)


def _prep_layer_weights(w_in, q_norm, kv_norm, w_uq, w_uk, w_uv, conv_w, w_bm, w_bc, w_bs, w_o, rw, rb):
    d = w_in.shape[0]

    def z(rows, n):
        return jnp.zeros((rows, n), F32)

    splits = (MLA_QR, MLA_KVR, MLA_DR, CONV_DIM, CONV_DIM, CONV_DIM, SB_H * SB_D, SB_G * SB_D, SB_G * SB_D,
              3 * D_MODEL)
    parts, start = [], 0
    for width in splits:
        parts.append(w_in[:, start:start + width])
        start += width
    cq, ckv, kr, cb, cc, cx, sq, sk, sv, gates = parts
    half = MLA_DR // 2
    sqh = sq.reshape(d, SB_H, SB_D)
    zq = jnp.zeros_like(sqh)
    group0 = (jnp.arange(SB_H) // (SB_H // SB_G) == 0)[None, :, None]
    sq_pad = jnp.where(group0, jnp.concatenate([sqh, zq], -1), jnp.concatenate([zq, sqh], -1))
    sq_pad = sq_pad.reshape(d, SB_H * HEAD_PAD)
    kr_rot = jnp.concatenate([-kr[:, half:], kr[:, :half]], axis=1)
    tail = HEAD_PAD - MLA_DN - MLA_DR
    kra = jnp.concatenate([z(d, MLA_DN), kr, z(d, tail)], axis=1)
    krb = jnp.concatenate([z(d, MLA_DN), kr_rot, z(d, tail)], axis=1)
    w_h = jnp.concatenate([gates, sq_pad, cb, cc, cx, ckv, sk, sv, kra, krb, z(d, C_CQ - C_KRB - LANES), cq],
                          axis=1).astype(BF16)

    qn_, qr_ = w_uq[:, :, :MLA_DN], w_uq[:, :, MLA_DN:]
    qr_rot = jnp.concatenate([-qr_[:, :, half:], qr_[:, :, :half]], axis=-1)
    zn = jnp.zeros((MLA_QR, MLA_H, MLA_DN), F32)
    zt = jnp.zeros((MLA_QR, MLA_H, tail), F32)
    wuq_a = jnp.concatenate([qn_, qr_, zt], axis=-1).reshape(MLA_QR, MLA_H * HEAD_PAD)
    wuq_b = jnp.concatenate([zn, qr_rot, zt], axis=-1).reshape(MLA_QR, MLA_H * HEAD_PAD)
    wuq = jnp.concatenate([wuq_a, wuq_b], axis=1).astype(BF16)
    zk = jnp.zeros((MLA_KVR, MLA_H, HEAD_PAD - MLA_DN), F32)
    wuk = jnp.concatenate([w_uk, zk], axis=-1).reshape(MLA_KVR, MLA_H * HEAD_PAD).astype(BF16)
    wuv = jnp.concatenate([w_uv, zk], axis=-1).reshape(MLA_KVR, MLA_H * HEAD_PAD).astype(BF16)

    eye = jnp.eye(MLA_DR, dtype=F32)
    top = jnp.concatenate([jnp.transpose(w_uk, (1, 2, 0)), jnp.zeros((MLA_H, MLA_DN, LANES), F32)], axis=-1)
    mid = jnp.concatenate([jnp.zeros((MLA_H, MLA_DR, MLA_KVR), F32),
                           jnp.broadcast_to(eye, (MLA_H, MLA_DR, MLA_DR)),
                           jnp.zeros((MLA_H, MLA_DR, LANES - MLA_DR), F32)], axis=-1)
    bot = jnp.zeros((MLA_H, tail, MLA_KVR + LANES), F32)
    w_qext = jnp.concatenate([top, mid, bot], axis=1).astype(BF16)
    w_uvd = jnp.concatenate([jnp.transpose(w_uv, (1, 0, 2)),
                             jnp.zeros((MLA_H, MLA_KVR, HEAD_PAD - MLA_DV), F32)], axis=-1).astype(BF16)

    wbm = jnp.concatenate([w_bm.reshape(MLA_H, MLA_DV, D_MODEL),
                           jnp.zeros((MLA_H, HEAD_PAD - MLA_DV, D_MODEL), F32)], axis=1)
    wbm = wbm.reshape(MLA_H * HEAD_PAD, D_MODEL).astype(BF16)
    wsh = w_bs.reshape(SB_H, SB_D, D_MODEL)
    zs = jnp.zeros_like(wsh)
    wbs = jnp.where(group0.reshape(SB_H, 1, 1), jnp.concatenate([wsh, zs], 1), jnp.concatenate([zs, wsh], 1))
    wbs = wbs.reshape(SB_H * HEAD_PAD, D_MODEL).astype(BF16)
    rwp = jnp.concatenate([rw, z(d, LANES - N_EXPERTS)], axis=1)
    rwh = rwp.astype(BF16)
    rwl = (rwp - rwh.astype(F32)).astype(BF16)
    rbp = jnp.concatenate([rb, jnp.full((LANES - N_EXPERTS,), NEG_BIG, F32)]).reshape(1, LANES)
    cw8 = jnp.concatenate([conv_w, jnp.zeros((8 - conv_w.shape[0], CONV_DIM), F32)], axis=0)
    return dict(w_h=w_h, qn=q_norm.reshape(1, -1), kvn=kv_norm.reshape(1, -1), wuq=wuq, wuk=wuk, wuv=wuv, cw=cw8,
                w_qext=w_qext, w_uvd=w_uvd, wbm=wbm, wbc=w_bc.astype(BF16), wbs=wbs, wo=w_o.astype(BF16),
                rwh=rwh, rwl=rwl, rb=rbp)


def _rope_tables(pos):
    half = MLA_DR // 2
    freq = ROPE_THETA ** (-jnp.arange(half, dtype=F32) / half)
    ang = pos.astype(F32)[:, None] * freq
    n = pos.shape[0]
    ones = jnp.ones((n, MLA_DN), F32)
    tail = HEAD_PAD - MLA_DN - MLA_DR
    cos_t = jnp.concatenate([ones, jnp.cos(ang), jnp.cos(ang), jnp.ones((n, tail), F32)], axis=1)
    sin_t = jnp.concatenate([jnp.zeros((n, MLA_DN), F32), jnp.sin(ang), jnp.sin(ang), jnp.zeros((n, tail), F32)],
                            axis=1)
    return cos_t, sin_t


def _group_tail(xg, h, om, oc, osb, lw, lay, experts, ln1, ln2, *, tm, chunk, alpha):
    x1, x1p, idx8, gw8 = _merge(om, oc, osb, h, xg, lw["wbm"], lw["wbc"], lw["wbs"], lw["wo"], ln1[0], ln1[1],
                                lw["rwh"], lw["rwl"], lw["rb"], tm=tm, alpha=alpha)
    t = xg.shape[0]
    n_chunks = t // chunk
    idx = idx8[:, :TOP_K].reshape(n_chunks, chunk, TOP_K)
    gw = gw8[:, :TOP_K].reshape(n_chunks, chunk, TOP_K)
    tile_e, n_tiles, src, nval, tok, gws = jax.vmap(_route)(idx, gw)
    pack_rows = D_MODEL // 2 // LANES
    x1p = x1p.reshape(n_chunks, chunk * pack_rows, LANES)
    wg, bg, wu, bu, wd, bd = experts
    outs = [_moe(tile_e[c], n_tiles[c], src[c], nval[c], tok[c], lay, gws[c], x1p[c], wg, bg, wu, bu, wd, bd)
            for c in range(n_chunks)]
    keep = chunk * (D_MODEL // LANES)
    ffn2 = outs[0] if n_chunks == 1 else jnp.concatenate([o[:keep] for o in outs], axis=0)
    return _ln2(ffn2, x1, ln2[0], ln2[1], tm=tm, alpha=alpha)


def kernel(x_prompt, x_sample, cache_mla_ckv, cache_mla_krope, cache_sb_k, cache_sb_v, state_conv, page_table, w_in, mla_q_norm, mla_kv_norm, mla_w_uq, mla_w_uk, mla_w_uv, conv_w, w_branch_mla, w_branch_conv, w_branch_sb, w_o, ln1_g, ln1_b, router_w, router_b, exp_w_gate, exp_b_gate, exp_w_up, exp_b_up, exp_w_down, exp_b_down, ln2_g, ln2_b):
    depth = w_in.shape[0]
    batch, seq, d = x_prompt.shape
    nb = x_sample.shape[0]
    n_pages = page_table.shape[1]
    past = n_pages * PAGE
    alpha = (2 * depth) ** 0.25
    tp = batch * seq

    n_pool = cache_sb_k.shape[1]
    c_krt = jnp.transpose(cache_mla_krope, (0, 1, 3, 2))
    c_kt = jnp.transpose(cache_sb_k, (0, 1, 3, 4, 2)).reshape(depth, n_pool, SB_G * SB_D, PAGE)
    c_vt = jnp.transpose(cache_sb_v, (0, 1, 3, 4, 2)).reshape(depth, n_pool, SB_G * SB_D, PAGE)
    pt_flat = page_table.reshape(-1).astype(jnp.int32)

    experts = (exp_w_gate.astype(BF16), exp_b_gate.reshape(depth, N_EXPERTS, 1, d),
               exp_w_up.astype(BF16), exp_b_up.reshape(depth, N_EXPERTS, 1, d),
               exp_w_down.astype(BF16), exp_b_down.reshape(depth, N_EXPERTS, 1, d))

    cos_p, sin_p = _rope_tables(jnp.arange(seq))
    cos_s, sin_s = _rope_tables(jnp.full((nb,), past))

    xp = x_prompt.reshape(tp, d)
    xs = x_sample.reshape(nb, d)
    new_p = [[], [], [], [], []]
    new_s = [[], [], [], [], []]
    for l in range(depth):
        lay = jnp.full((1,), l, jnp.int32)
        lw = _prep_layer_weights(w_in[l], mla_q_norm[l], mla_kv_norm[l], mla_w_uq[l], mla_w_uk[l], mla_w_uv[l],
                                 conv_w[l], w_branch_mla[l], w_branch_conv[l], w_branch_sb[l], w_o[l],
                                 router_w[l], router_b[l])
        ln1 = (ln1_g[l].reshape(1, d), ln1_b[l].reshape(1, d))
        ln2 = (ln2_g[l].reshape(1, d), ln2_b[l].reshape(1, d))

        h = _linear(xp, lw["w_h"], F32, 1024, 768)
        ckv, kr, q, k, v, oc, tail, sq, sk, sv = _prep(
            h, cos_p, sin_p, lw["qn"], lw["kvn"], lw["wuq"], lw["wuk"], lw["wuv"], lw["cw"], None,
            tm=512, decode=False, tiles_per_seq=seq // 512)
        om = _mla_attn(q, k, v, batch=batch, seq=seq, tq=512)
        osb = _sb_attn(sq, sk, sv, batch=batch, seq=seq, tq=256)
        new_p[0].append(ckv.reshape(batch, seq, MLA_KVR))
        new_p[1].append(kr.reshape(batch, seq, MLA_DR))
        new_p[2].append(h[:, C_SBK:C_SBK + LANES].reshape(batch, seq, SB_G, SB_D))
        new_p[3].append(h[:, C_SBV:C_SBV + LANES].reshape(batch, seq, SB_G, SB_D))
        new_p[4].append(tail.reshape(batch, seq // 512, 8, CONV_DIM)[:, -1, 6:, :])
        xp = _group_tail(xp, h, om, oc, osb, lw, lay, experts, ln1, ln2, tm=512, chunk=seq, alpha=alpha)

        hs = _linear(xs, lw["w_h"], F32, nb, 768)
        st = state_conv[l]
        ckv_s, kr_s, q_s, _, _, oc_s, u_s, sq_s, _, _ = _prep(
            hs, cos_s, sin_s, lw["qn"], lw["kvn"], lw["wuq"], lw["wuk"], lw["wuv"], lw["cw"],
            (st[:, 0, :], st[:, 1, :]), tm=nb, decode=True, tiles_per_seq=1)
        qe = _headwise_linear(q_s, lw["w_qext"], BF16).reshape(nb, MLA_H, MLA_KVR + LANES)
        olat, osb_s = _decode_attn(pt_flat, lay, qe, sq_s.reshape(nb, SB_H, HEAD_PAD),
                                   ckv_s.reshape(nb, 1, MLA_KVR), kr_s.reshape(nb, 1, MLA_DR),
                                   cache_mla_ckv, c_krt, c_kt, c_vt, pages_per_step=min(16, n_pages))
        om_s = _headwise_linear(olat.reshape(nb, MLA_H * MLA_KVR), lw["w_uvd"], BF16)
        new_s[0].append(ckv_s.reshape(nb, 1, MLA_KVR))
        new_s[1].append(kr_s.reshape(nb, 1, MLA_DR))
        new_s[2].append(hs[:, C_SBK:C_SBK + LANES].reshape(nb, 1, SB_G, SB_D))
        new_s[3].append(hs[:, C_SBV:C_SBV + LANES].reshape(nb, 1, SB_G, SB_D))
        new_s[4].append(jnp.stack([st[:, 1, :], u_s], axis=1))
        xs = _group_tail(xs, hs, om_s, oc_s, osb_s.reshape(nb, SB_H * HEAD_PAD), lw, lay, experts, ln1, ln2,
                         tm=nb, chunk=nb, alpha=alpha)

    outs_p = [jnp.stack(a) for a in new_p]
    outs_s = [jnp.stack(a) for a in new_s]
    return (xp.reshape(batch, seq, d), xs.reshape(nb, 1, d), *outs_p, *outs_s)
```

```python
import functools

import jax
import jax.numpy as jnp
from jax import lax
from jax.experimental import pallas as pl
from jax.experimental.pallas import tpu as pltpu

F32 = jnp.float32
BF16 = jnp.bfloat16
U32 = jnp.uint32

D_MODEL = 1024
PAGE = 128
MLA_H = 8
MLA_DN = 64
MLA_DR = 32
MLA_DV = 64
MLA_QR = 384
MLA_KVR = 256
MLA_SCALE = (MLA_DN + MLA_DR) ** -0.5
ROPE_THETA = 10000.0
CONV_DIM = 512
SB_H = 8
SB_G = 2
SB_D = 64
SB_SCALE = SB_D ** -0.5
N_EXPERTS = 32
TOP_K = 4
SWIGLU_LIMIT = 7.0
SWIGLU_ALPHA = 1.702
LN_EPS = 1e-5
RMS_EPS = 1e-6
LANES = 128
HEAD_PAD = 128
MOE_TILE = 256
NEG_BIG = -1e30
SB_DEAD_LOG = -120.0

C_GATE, C_SBQ, C_CB, C_CC, C_CX, C_CKV, C_SBK, C_SBV, C_KRA, C_KRB, C_CQ = (
    0, 3072, 4096, 4608, 5120, 5632, 5888, 6016, 6144, 6272, 6528)
H_COLS = 6912
VMEM_LIMIT = 56 * 1024 * 1024


def _cp(sem, vmem=VMEM_LIMIT):
    return pltpu.CompilerParams(dimension_semantics=sem, vmem_limit_bytes=vmem)


def _linear_kernel(x_ref, w_ref, o_ref, xb_ref):
    @pl.when(pl.program_id(1) == 0)
    def _():
        xb_ref[...] = x_ref[...].astype(BF16)

    o_ref[...] = jnp.dot(xb_ref[...], w_ref[...], preferred_element_type=F32).astype(o_ref.dtype)


def _linear(x, w, out_dtype, tm, tn):
    m, k = x.shape
    n = w.shape[1]
    return pl.pallas_call(
        _linear_kernel,
        grid=(m // tm, n // tn),
        in_specs=[pl.BlockSpec((tm, k), lambda i, j: (i, 0)),
                  pl.BlockSpec((k, tn), lambda i, j: (0, j))],
        out_specs=pl.BlockSpec((tm, tn), lambda i, j: (i, j)),
        out_shape=jax.ShapeDtypeStruct((m, n), out_dtype),
        scratch_shapes=[pltpu.VMEM((tm, k), BF16)],
        compiler_params=_cp(("arbitrary", "arbitrary")),
        name="linear",
    )(x, w)


def _headwise_kernel(x_ref, w_ref, o_ref):
    o_ref[...] = jnp.dot(x_ref[...], w_ref[0], preferred_element_type=F32).astype(o_ref.dtype)


def _headwise_linear(x, w, out_dtype):
    m = x.shape[0]
    h, k, n = w.shape
    return pl.pallas_call(
        _headwise_kernel,
        grid=(h,),
        in_specs=[pl.BlockSpec((m, k), lambda i: (0, i)),
                  pl.BlockSpec((1, k, n), lambda i: (i, 0, 0))],
        out_specs=pl.BlockSpec((m, n), lambda i: (0, i)),
        out_shape=jax.ShapeDtypeStruct((m, h * n), out_dtype),
        compiler_params=_cp(("arbitrary",)),
        name="headwise_linear",
    )(x, w)


def _rms(x, g):
    return x * lax.rsqrt(jnp.mean(x * x, axis=-1, keepdims=True) + RMS_EPS) * g


def _prep_kernel(cq_ref, ckv_ref, kra_ref, krb_ref, cb_ref, cc_ref, cx_ref, pa_ref, pb_ref,
                 sq_ref, sk_ref, sv_ref, cos_ref, sin_ref, qn_ref, kvn_ref, wuq_ref, wuk_ref, wuv_ref, cw_ref,
                 ckv_o, kr_o, q_o, k_o, v_o, oc_o, tail_o, sq_o, sk_o, sv_o, *scratch, decode, tiles_per_seq):
    tm = cq_ref.shape[0]
    cos = cos_ref[...]
    sin = sin_ref[...]
    cos8 = jnp.concatenate([cos] * MLA_H, axis=1)
    sin8 = jnp.concatenate([sin] * MLA_H, axis=1)

    cqn = _rms(cq_ref[...], qn_ref[...]).astype(BF16)
    qq = jnp.dot(cqn, wuq_ref[...], preferred_element_type=F32)
    width = MLA_H * HEAD_PAD
    q_o[...] = (qq[:, :width] * cos8 + qq[:, width:] * sin8).astype(BF16)

    ckv = _rms(ckv_ref[...], kvn_ref[...])
    ckv_o[...] = ckv
    ckvb = ckv.astype(BF16)
    krp = kra_ref[...] * cos + krb_ref[...] * sin
    kr_o[...] = krp[:, MLA_DN:MLA_DN + MLA_DR]
    kn = jnp.dot(ckvb, wuk_ref[...], preferred_element_type=F32)
    k_o[...] = (kn + jnp.concatenate([krp] * MLA_H, axis=1)).astype(BF16)
    v_o[...] = jnp.dot(ckvb, wuv_ref[...], preferred_element_type=F32).astype(BF16)

    u = cc_ref[...] * cx_ref[...]
    w0 = cw_ref[0:1, :]
    w1 = cw_ref[1:2, :]
    w2 = cw_ref[2:3, :]
    if decode:
        conv = w0 * pa_ref[...] + w1 * pb_ref[...] + w2 * u
        tail_o[...] = u
    else:
        ext_ref, = scratch
        first = (pl.program_id(0) % tiles_per_seq) == 0
        prev = pa_ref[...] * pb_ref[...]
        ext_ref[0:8, :] = jnp.where(first, 0.0, prev)
        ext_ref[8:, :] = u
        conv = w0 * ext_ref[pl.ds(6, tm), :] + w1 * ext_ref[pl.ds(7, tm), :] + w2 * u
        tail_o[...] = u[tm - 8:, :]
    oc_o[...] = (cb_ref[...] * conv).astype(BF16)

    sq_o[...] = sq_ref[...].astype(BF16)
    sk_o[...] = sk_ref[...].astype(BF16)
    sv_o[...] = sv_ref[...].astype(BF16)


def _prep(h, cos_t, sin_t, qn, kvn, wuq, wuk, wuv, cw, state, *, tm, decode, tiles_per_seq):
    t = h.shape[0]
    nt = t // tm
    n_tab = cos_t.shape[0] // tm

    def hs(width, col):
        return pl.BlockSpec((tm, width), lambda i: (i, col // width))

    def full(a):
        return pl.BlockSpec(a.shape, lambda i: (0,) * a.ndim)

    if decode:
        pa, pb = state
        pa_spec = pb_spec = pl.BlockSpec((tm, CONV_DIM), lambda i: (i, 0))
        tail_shape, tail_spec = (t, CONV_DIM), pl.BlockSpec((tm, CONV_DIM), lambda i: (i, 0))
        scratch = []
    else:
        pa = pb = h
        r8 = tm // 8
        pa_spec = pl.BlockSpec((8, CONV_DIM), lambda i: (jnp.maximum(i * r8 - 1, 0), C_CC // CONV_DIM))
        pb_spec = pl.BlockSpec((8, CONV_DIM), lambda i: (jnp.maximum(i * r8 - 1, 0), C_CX // CONV_DIM))
        tail_shape, tail_spec = (nt * 8, CONV_DIM), pl.BlockSpec((8, CONV_DIM), lambda i: (i, 0))
        scratch = [pltpu.VMEM((tm + 8, CONV_DIM), F32)]
    tab_spec = pl.BlockSpec((tm, LANES), lambda i: (i % n_tab, 0))
    wide = MLA_H * HEAD_PAD
    out_shape = [
        jax.ShapeDtypeStruct((t, MLA_KVR), F32), jax.ShapeDtypeStruct((t, MLA_DR), F32),
        jax.ShapeDtypeStruct((t, wide), BF16), jax.ShapeDtypeStruct((t, wide), BF16),
        jax.ShapeDtypeStruct((t, wide), BF16), jax.ShapeDtypeStruct((t, CONV_DIM), BF16),
        jax.ShapeDtypeStruct(tail_shape, F32), jax.ShapeDtypeStruct((t, SB_H * HEAD_PAD), BF16),
        jax.ShapeDtypeStruct((t, LANES), BF16), jax.ShapeDtypeStruct((t, LANES), BF16)]

    def os(width):
        return pl.BlockSpec((tm, width), lambda i: (i, 0))

    out_specs = [os(MLA_KVR), os(MLA_DR), os(wide), os(wide), os(wide), os(CONV_DIM), tail_spec,
                 os(SB_H * HEAD_PAD), os(LANES), os(LANES)]
    return pl.pallas_call(
        functools.partial(_prep_kernel, decode=decode, tiles_per_seq=tiles_per_seq),
        grid=(nt,),
        in_specs=[hs(MLA_QR, C_CQ), hs(MLA_KVR, C_CKV), hs(LANES, C_KRA), hs(LANES, C_KRB),
                  hs(CONV_DIM, C_CB), hs(CONV_DIM, C_CC), hs(CONV_DIM, C_CX), pa_spec, pb_spec,
                  hs(SB_H * HEAD_PAD, C_SBQ), hs(LANES, C_SBK), hs(LANES, C_SBV),
                  tab_spec, tab_spec, full(qn), full(kvn), full(wuq), full(wuk), full(wuv), full(cw)],
        out_specs=out_specs,
        out_shape=out_shape,
        scratch_shapes=scratch,
        compiler_params=_cp(("arbitrary",)),
        name="prep_decode" if decode else "prep",
    )(h, h, h, h, h, h, h, pa, pb, h, h, h, cos_t, sin_t, qn, kvn, wuq, wuk, wuv, cw)


def _mla_attn_kernel(q_ref, k_ref, v_ref, o_ref, *, tq):
    qi = pl.program_id(2)
    q = q_ref[...]

    def scores(j):
        off = pl.multiple_of(j * tq, tq)
        k = k_ref[pl.ds(off, tq), :]
        v = v_ref[pl.ds(off, tq), :]
        s = lax.dot_general(q, k, (((1,), (1,)), ((), ())), preferred_element_type=F32) * MLA_SCALE
        return s, v

    def update(carry, s, v):
        m, l, acc = carry
        m_new = jnp.maximum(m, jnp.max(s, axis=-1, keepdims=True))
        alpha = jnp.exp(m - m_new)
        p = jnp.exp(s - m_new)
        l = alpha * l + jnp.sum(p, axis=-1, keepdims=True)
        acc = alpha * acc + jnp.dot(p.astype(BF16), v, preferred_element_type=F32)
        return m_new, l, acc

    def body(j, carry):
        s, v = scores(j)
        return update(carry, s, v)

    init = (jnp.full((tq, 1), NEG_BIG, F32), jnp.zeros((tq, 1), F32), jnp.zeros((tq, HEAD_PAD), F32))
    carry = lax.fori_loop(0, qi, body, init)
    s, v = scores(qi)
    row = lax.broadcasted_iota(jnp.int32, (tq, tq), 0)
    col = lax.broadcasted_iota(jnp.int32, (tq, tq), 1)
    s = jnp.where(col <= row, s, NEG_BIG)
    _, l, acc = update(carry, s, v)
    o_ref[...] = (acc / l).astype(o_ref.dtype)


def _mla_attn(q, k, v, *, batch, seq, tq):
    nq = seq // tq
    qspec = pl.BlockSpec((tq, HEAD_PAD), lambda b, h, i: (b * nq + i, h))
    kspec = pl.BlockSpec((seq, HEAD_PAD), lambda b, h, i: (b, h))
    return pl.pallas_call(
        functools.partial(_mla_attn_kernel, tq=tq),
        grid=(batch, MLA_H, nq),
        in_specs=[qspec, kspec, kspec],
        out_specs=qspec,
        out_shape=jax.ShapeDtypeStruct(q.shape, BF16),
        compiler_params=_cp(("arbitrary", "arbitrary", "arbitrary")),
        name="mla_attn",
    )(q, k, v)


def _log_sigmoid(z):
    return jnp.minimum(z, 0.0) - jnp.log1p(jnp.exp(-jnp.abs(z)))


def _strict_suffix(lk, upper):
    hi = lk.astype(BF16)
    lo = (lk - hi.astype(F32)).astype(BF16)
    return (jnp.dot(hi, upper, preferred_element_type=F32) + jnp.dot(lo, upper, preferred_element_type=F32))


def _upper(n):
    j = lax.broadcasted_iota(jnp.int32, (n, n), 0)
    s = lax.broadcasted_iota(jnp.int32, (n, n), 1)
    return jnp.where(j > s, 1.0, 0.0).astype(BF16)


def _sb_attn_kernel(q_ref, k_ref, v_ref, o_ref, *, tq):
    qi = pl.program_id(2)
    q = q_ref[...]
    upper = _upper(tq)

    def block(j, carry, acc, diag):
        off = pl.multiple_of(j * tq, tq)
        k = k_ref[pl.ds(off, tq), :]
        v = v_ref[pl.ds(off, tq), :]
        z = lax.dot_general(q, k, (((1,), (1,)), ((), ())), preferred_element_type=F32) * SB_SCALE
        lb = _log_sigmoid(z)
        lk = lb - z
        if diag:
            row = lax.broadcasted_iota(jnp.int32, (tq, tq), 0)
            col = lax.broadcasted_iota(jnp.int32, (tq, tq), 1)
            valid = col < row
            lk = jnp.where(valid, lk, 0.0)
        a = jnp.exp(lb + _strict_suffix(lk, upper) + carry)
        if diag:
            a = jnp.where(valid, a, 0.0)
        acc = acc + jnp.dot(a.astype(BF16), v, preferred_element_type=F32)
        carry = carry + jnp.sum(lk, axis=-1, keepdims=True)
        return carry, acc

    carry, acc = block(qi, jnp.zeros((tq, 1), F32), jnp.zeros((tq, HEAD_PAD), F32), True)

    def cond(st):
        return jnp.logical_and(st[0] < qi, jnp.max(st[1]) > SB_DEAD_LOG)

    def body(st):
        t, carry, acc = st
        carry, acc = block(qi - 1 - t, carry, acc, False)
        return t + 1, carry, acc

    _, carry, acc = lax.while_loop(cond, body, (jnp.int32(0), carry, acc))
    o_ref[...] = acc.astype(o_ref.dtype)


def _sb_attn(q, k, v, *, batch, seq, tq):
    nq = seq // tq
    qspec = pl.BlockSpec((tq, HEAD_PAD), lambda b, h, i: (b * nq + i, h))
    kspec = pl.BlockSpec((seq, LANES), lambda b, h, i: (b, 0))
    return pl.pallas_call(
        functools.partial(_sb_attn_kernel, tq=tq),
        grid=(batch, SB_H, nq),
        in_specs=[qspec, kspec, kspec],
        out_specs=qspec,
        out_shape=jax.ShapeDtypeStruct(q.shape, BF16),
        compiler_params=_cp(("arbitrary", "arbitrary", "arbitrary")),
        name="sb_attn",
    )(q, k, v)


def _decode_kernel(pt_ref, lay_ref, qe_ref, qs_ref, cn_ref, kn_ref, *refs, pages_per_step, n_steps):
    del pt_ref, lay_ref
    p_ = pages_per_step
    caches = refs[:4 * p_]
    olat_ref, osb_ref, m_ref, l_ref, acc_ref, car_ref, accs_ref = refs[4 * p_:]
    j = pl.program_id(1)
    qe = qe_ref[0]
    ql = qe[:, :MLA_KVR]
    qr = qe[:, MLA_KVR:MLA_KVR + MLA_DR]
    qs = qs_ref[0]

    @pl.when(j == 0)
    def _():
        cn = cn_ref[0]
        kn = kn_ref[0]
        s0 = (jnp.sum(ql.astype(F32) * cn, axis=-1, keepdims=True)
              + jnp.sum(qr.astype(F32) * kn, axis=-1, keepdims=True)) * MLA_SCALE
        m_ref[...] = s0
        l_ref[...] = jnp.ones_like(l_ref)
        acc_ref[...] = jnp.broadcast_to(cn, acc_ref.shape)
        car_ref[...] = jnp.zeros_like(car_ref)
        accs_ref[...] = jnp.zeros_like(accs_ref)

    upper = _upper(PAGE)
    contract_last = (((1,), (1,)), ((), ()))
    ckv = jnp.concatenate([caches[i][0, 0].astype(BF16) for i in range(p_)], axis=0)
    kr_t = jnp.concatenate([caches[p_ + i][0, 0].astype(BF16) for i in range(p_)], axis=1)
    s = (lax.dot_general(ql, ckv, contract_last, preferred_element_type=F32)
         + jnp.dot(qr, kr_t, preferred_element_type=F32)) * MLA_SCALE
    m = m_ref[...]
    m_new = jnp.maximum(m, jnp.max(s, axis=-1, keepdims=True))
    alpha = jnp.exp(m - m_new)
    p = jnp.exp(s - m_new)
    l_ref[...] = alpha * l_ref[...] + jnp.sum(p, axis=-1, keepdims=True)
    acc_ref[...] = alpha * acc_ref[...] + jnp.dot(p.astype(BF16), ckv, preferred_element_type=F32)
    m_ref[...] = m_new

    def sb_page(i):
        k_t = caches[2 * p_ + i][0, 0].astype(BF16)
        v_t = caches[3 * p_ + i][0, 0].astype(BF16)
        z = jnp.dot(qs, k_t, preferred_element_type=F32) * SB_SCALE
        lb = _log_sigmoid(z)
        lk = lb - z
        a = jnp.exp(lb + _strict_suffix(lk, upper) + car_ref[...])
        accs_ref[...] += lax.dot_general(a.astype(BF16), v_t, contract_last, preferred_element_type=F32)
        car_ref[...] += jnp.sum(lk, axis=-1, keepdims=True)

    def sb_from(start, width):
        @pl.when(jnp.max(car_ref[...]) > SB_DEAD_LOG)
        def _():
            stop = min(start + width, p_)
            for i in range(start, stop):
                sb_page(i)
            if stop < p_:
                sb_from(stop, width if start == 0 else 2 * width)

    sb_from(0, 2)

    @pl.when(j == n_steps - 1)
    def _():
        olat_ref[0] = (acc_ref[...] / l_ref[...]).astype(olat_ref.dtype)
        osb_ref[0] = accs_ref[...].astype(osb_ref.dtype)


def _decode_attn(page_table, layer, qe, qs, ckv_new, kr_new, c_ckv, c_krt, c_kt, c_vt, *, pages_per_step):
    nb = qe.shape[0]
    n_pages = page_table.shape[0] // nb
    p_ = pages_per_step
    n_steps = n_pages // p_

    def cache_spec(shape, i):
        def imap(b, j, pt, lay):
            return (lay[0], pt[b * n_pages + (n_pages - 1 - (j * p_ + i))], 0, 0)
        return pl.BlockSpec((1, 1) + shape, imap)

    def seq_spec(shape):
        return pl.BlockSpec((1,) + shape, lambda b, j, pt, lay: (b, 0, 0))

    cache_specs, cache_args = [], []
    for arr in (c_ckv, c_krt, c_kt, c_vt):
        for i in range(p_):
            cache_specs.append(cache_spec(arr.shape[2:], i))
            cache_args.append(arr)
    grid_spec = pltpu.PrefetchScalarGridSpec(
        num_scalar_prefetch=2,
        grid=(nb, n_steps),
        in_specs=[seq_spec(qe.shape[1:]), seq_spec(qs.shape[1:]), seq_spec((1, MLA_KVR)), seq_spec((1, MLA_DR))]
        + cache_specs,
        out_specs=[seq_spec((MLA_H, MLA_KVR)), seq_spec((SB_H, HEAD_PAD))],
        scratch_shapes=[pltpu.VMEM((MLA_H, 1), F32), pltpu.VMEM((MLA_H, 1), F32), pltpu.VMEM((MLA_H, MLA_KVR), F32),
                        pltpu.VMEM((SB_H, 1), F32), pltpu.VMEM((SB_H, HEAD_PAD), F32)])
    return pl.pallas_call(
        functools.partial(_decode_kernel, pages_per_step=p_, n_steps=n_steps),
        grid_spec=grid_spec,
        out_shape=[jax.ShapeDtypeStruct((nb, MLA_H, MLA_KVR), BF16), jax.ShapeDtypeStruct((nb, SB_H, HEAD_PAD), BF16)],
        compiler_params=_cp(("arbitrary", "arbitrary")),
        name="decode_attn",
    )(page_table, layer, qe, qs, ckv_new, kr_new, *cache_args)


def _layer_norm(x, g, b):
    mu = jnp.mean(x, axis=-1, keepdims=True)
    xc = x - mu
    var = jnp.mean(xc * xc, axis=-1, keepdims=True)
    return xc * lax.rsqrt(var + LN_EPS) * g + b


def _merge_kernel(om_ref, oc_ref, os_ref, g0_ref, g1_ref, g2_ref, x_ref, wbm_ref, wbc_ref, wbs_ref, wo_ref,
                  lg_ref, lb_ref, rwh_ref, rwl_ref, rb_ref, x1_ref, xp_ref, idx_ref, gw_ref, *, alpha):
    tm = x_ref.shape[0]
    merged = (jax.nn.sigmoid(g0_ref[...]) * jnp.dot(om_ref[...], wbm_ref[...], preferred_element_type=F32)
              + jax.nn.sigmoid(g1_ref[...]) * jnp.dot(oc_ref[...], wbc_ref[...], preferred_element_type=F32)
              + jax.nn.sigmoid(g2_ref[...]) * jnp.dot(os_ref[...], wbs_ref[...], preferred_element_type=F32))
    mix = jnp.dot(merged.astype(BF16), wo_ref[...], preferred_element_type=F32)
    x1 = _layer_norm(alpha * x_ref[...] + mix, lg_ref[...], lb_ref[...])
    x1_ref[...] = x1
    xb = x1.astype(BF16)

    bits = pltpu.bitcast(xb.astype(F32), U32)
    half = D_MODEL // 2
    word = bits[:, half:] | (bits[:, :half] >> 16)
    for c in range(half // LANES):
        xp_ref[pl.ds(c, tm, stride=half // LANES), :] = word[:, c * LANES:(c + 1) * LANES]

    xl = (x1 - xb.astype(F32)).astype(BF16)
    logits = (jnp.dot(xb, rwh_ref[...], preferred_element_type=F32) + jnp.dot(xl, rwh_ref[...], preferred_element_type=F32)
              + jnp.dot(xb, rwl_ref[...], preferred_element_type=F32) + rb_ref[...])
    lane = lax.broadcasted_iota(jnp.int32, logits.shape, 1)
    vals, idxs = [], []
    for _ in range(TOP_K):
        mx = jnp.max(logits, axis=-1, keepdims=True)
        ix = jnp.min(jnp.where(logits == mx, lane, LANES), axis=-1, keepdims=True)
        vals.append(mx)
        idxs.append(ix)
        logits = jnp.where(lane == ix, NEG_BIG * 2, logits)
    es = [jnp.exp(v - vals[0]) for v in vals]
    den = es[0] + es[1] + es[2] + es[3]
    lane8 = lax.broadcasted_iota(jnp.int32, (tm, 8), 1)
    idx_out = jnp.zeros((tm, 8), jnp.int32)
    gw_out = jnp.zeros((tm, 8), F32)
    for kk in range(TOP_K):
        idx_out = jnp.where(lane8 == kk, idxs[kk], idx_out)
        gw_out = jnp.where(lane8 == kk, es[kk] / den, gw_out)
    idx_ref[...] = idx_out
    gw_ref[...] = gw_out


def _merge(om, oc, osb, h, x, wbm, wbc, wbs, wo, lg, lb, rwh, rwl, rb, *, tm, alpha):
    t = x.shape[0]

    def rs(width, col=0):
        return pl.BlockSpec((tm, width), lambda i: (i, col))

    def full(a):
        return pl.BlockSpec(a.shape, lambda i: (0,) * a.ndim)

    pack_rows = D_MODEL // 2 // LANES
    return pl.pallas_call(
        functools.partial(_merge_kernel, alpha=alpha),
        grid=(t // tm,),
        in_specs=[rs(om.shape[1]), rs(CONV_DIM), rs(osb.shape[1]), rs(D_MODEL, 0), rs(D_MODEL, 1), rs(D_MODEL, 2),
                  rs(D_MODEL), full(wbm), full(wbc), full(wbs), full(wo), full(lg), full(lb), full(rwh), full(rwl), full(rb)],
        out_specs=[rs(D_MODEL), pl.BlockSpec((tm * pack_rows, LANES), lambda i: (i, 0)), rs(8), rs(8)],
        out_shape=[jax.ShapeDtypeStruct((t, D_MODEL), F32), jax.ShapeDtypeStruct((t * pack_rows, LANES), U32),
                   jax.ShapeDtypeStruct((t, 8), jnp.int32), jax.ShapeDtypeStruct((t, 8), F32)],
        compiler_params=_cp(("arbitrary",)),
        name="merge",
    )(om, oc, osb, h, h, h, x, wbm, wbc, wbs, wo, lg, lb, rwh, rwl, rb)


def _moe_kernel(te_ref, nt_ref, src_ref, nv_ref, tok_ref, lay_ref, gws_ref, x2_ref,
                wg_ref, bg_ref, wu_ref, bu_ref, wd_ref, bd_ref, out_ref, xs_ref, ys_ref):
    del te_ref, lay_ref
    s = pl.program_id(0)
    pack_rows = D_MODEL // 2 // LANES
    out_rows = D_MODEL // LANES
    n_assign = tok_ref.shape[0]
    n_tok = x2_ref.shape[0] // pack_rows
    batch = 8

    @pl.when(s == 0)
    def _():
        out_ref[...] = jnp.zeros_like(out_ref)

    @pl.when(s < nt_ref[0])
    def _():
        src = src_ref[s]
        nv = nv_ref[s]

        def gather(r, c):
            t = jnp.where(r < nv, tok_ref[jnp.minimum(src + r, n_assign - 1)], 0)
            xs_ref[pl.ds(pl.multiple_of(r * pack_rows, pack_rows), pack_rows), :] = (
                x2_ref[pl.ds(pl.multiple_of(t * pack_rows, pack_rows), pack_rows), :])
            return c

        lax.fori_loop(0, MOE_TILE, gather, 0, unroll=8)
        lo, hi = [], []
        for c in range(pack_rows):
            w = xs_ref[pl.ds(c, MOE_TILE, stride=pack_rows), :]
            hi.append(pltpu.bitcast(w << 16, F32).astype(BF16))
            lo.append(pltpu.bitcast(w & jnp.uint32(0xFFFF0000), F32).astype(BF16))
        x = jnp.concatenate(hi + lo, axis=1)
        gt = jnp.minimum(jnp.dot(x, wg_ref[0, 0], preferred_element_type=F32) + bg_ref[0, 0], SWIGLU_LIMIT)
        up = jnp.clip(jnp.dot(x, wu_ref[0, 0], preferred_element_type=F32) + bu_ref[0, 0],
                      -SWIGLU_LIMIT, SWIGLU_LIMIT)
        hid = gt * jax.nn.sigmoid(SWIGLU_ALPHA * gt) * (up + 1.0)
        y = jnp.dot(hid.astype(BF16), wd_ref[0, 0], preferred_element_type=F32) + bd_ref[0, 0]
        for c in range(out_rows):
            ys_ref[pl.ds(c, MOE_TILE, stride=out_rows), :] = y[:, c * LANES:(c + 1) * LANES]

        def scatter(g, c):
            pending = []
            for i in range(batch):
                r = g * batch + i
                j = jnp.minimum(src + r, n_assign - 1)
                t = jnp.where(r < nv, tok_ref[j], n_tok)
                dst = pl.ds(pl.multiple_of(t * out_rows, out_rows), out_rows)
                row = ys_ref[pl.ds(pl.multiple_of(r * out_rows, out_rows), out_rows), :]
                pending.append((dst, out_ref[dst, :] + row * gws_ref[j]))
            for dst, val in pending:
                out_ref[dst, :] = val
            return c

        lax.fori_loop(0, MOE_TILE // batch, scatter, 0, unroll=2)


def _moe(tile_e, n_tiles, src, nval, tok, layer, gws, x2, wg, bg, wu, bu, wd, bd):
    n_tok = x2.shape[0] // (D_MODEL // 2 // LANES)
    nt = tile_e.shape[0]

    def wspec():
        return pl.BlockSpec((1, 1, D_MODEL, D_MODEL), lambda s, te, n, sr, nv, tk, lay: (lay[0], te[s], 0, 0))

    def bspec():
        return pl.BlockSpec((1, 1, 1, D_MODEL), lambda s, te, n, sr, nv, tk, lay: (lay[0], te[s], 0, 0))

    grid_spec = pltpu.PrefetchScalarGridSpec(
        num_scalar_prefetch=6,
        grid=(nt,),
        in_specs=[pl.BlockSpec(memory_space=pltpu.SMEM), pl.BlockSpec(memory_space=pltpu.VMEM),
                  wspec(), bspec(), wspec(), bspec(), wspec(), bspec()],
        out_specs=pl.BlockSpec(memory_space=pltpu.VMEM),
        scratch_shapes=[pltpu.VMEM((MOE_TILE * (D_MODEL // 2 // LANES), LANES), U32),
                        pltpu.VMEM((MOE_TILE * (D_MODEL // LANES), LANES), F32)])
    return pl.pallas_call(
        _moe_kernel,
        grid_spec=grid_spec,
        out_shape=jax.ShapeDtypeStruct(((n_tok + 1) * (D_MODEL // LANES), LANES), F32),
        compiler_params=_cp(("arbitrary",)),
        name="moe",
    )(tile_e, n_tiles, src, nval, tok, layer, gws, x2, wg, bg, wu, bu, wd, bd)


def _route(idx, gw):
    n_tok = idx.shape[0]
    n_assign = n_tok * TOP_K
    n_tiles_max = n_assign // MOE_TILE + N_EXPERTS
    flat_e = idx.reshape(n_assign)
    _, order, gws = lax.sort((flat_e, jnp.arange(n_assign, dtype=jnp.int32), gw.reshape(n_assign)),
                             num_keys=1, is_stable=True)
    tok = order // TOP_K
    experts = jnp.arange(N_EXPERTS, dtype=jnp.int32)
    counts = jnp.sum((flat_e[:, None] == experts[None, :]).astype(jnp.int32), axis=0)
    starts = jnp.cumsum(counts) - counts
    n_tile = (counts + MOE_TILE - 1) // MOE_TILE
    tile_end = jnp.cumsum(n_tile)
    tile_start = tile_end - n_tile
    total = tile_end[-1]
    slot = jnp.minimum(jnp.arange(n_tiles_max, dtype=jnp.int32), total - 1)
    tile_e = jnp.sum((tile_end[None, :] <= slot[:, None]).astype(jnp.int32), axis=1)
    onehot = tile_e[:, None] == experts[None, :]

    def pick(v):
        return jnp.sum(jnp.where(onehot, v[None, :], 0), axis=1)

    k = slot - pick(tile_start)
    src = pick(starts) + k * MOE_TILE
    nval = jnp.clip(pick(counts) - k * MOE_TILE, 0, MOE_TILE)
    return tile_e, total.reshape(1), src, nval, tok, gws


def _ln2_kernel(f_ref, x_ref, g_ref, b_ref, o_ref, *, alpha):
    tm = x_ref.shape[0]
    rows = D_MODEL // LANES
    ffn = jnp.concatenate([f_ref[pl.ds(c, tm, stride=rows), :] for c in range(rows)], axis=1)
    o_ref[...] = _layer_norm(alpha * x_ref[...] + ffn, g_ref[...], b_ref[...])


def _ln2(ffn2, x1, g, b, *, tm, alpha):
    t = x1.shape[0]
    rows = D_MODEL // LANES
    return pl.pallas_call(
        functools.partial(_ln2_kernel, alpha=alpha),
        grid=(t // tm,),
        in_specs=[pl.BlockSpec((tm * rows, LANES), lambda i: (i, 0)),
                  pl.BlockSpec((tm, D_MODEL), lambda i: (i, 0)),
                  pl.BlockSpec(g.shape, lambda i: (0, 0)), pl.BlockSpec(b.shape, lambda i: (0, 0))],
        out_specs=pl.BlockSpec((tm, D_MODEL), lambda i: (i, 0)),
        out_shape=jax.ShapeDtypeStruct((t, D_MODEL), F32),
        compiler_params=_cp(("arbitrary",)),
        name="ln2",
    )(ffn2, x1, g, b)


def _prep_layer_weights(w_in, q_norm, kv_norm, w_uq, w_uk, w_uv, conv_w, w_bm, w_bc, w_bs, w_o, rw, rb):
    d = w_in.shape[0]

    def z(rows, n):
        return jnp.zeros((rows, n), F32)

    splits = (MLA_QR, MLA_KVR, MLA_DR, CONV_DIM, CONV_DIM, CONV_DIM, SB_H * SB_D, SB_G * SB_D, SB_G * SB_D,
              3 * D_MODEL)
    parts, start = [], 0
    for width in splits:
        parts.append(w_in[:, start:start + width])
        start += width
    cq, ckv, kr, cb, cc, cx, sq, sk, sv, gates = parts
    half = MLA_DR // 2
    sqh = sq.reshape(d, SB_H, SB_D)
    zq = jnp.zeros_like(sqh)
    group0 = (jnp.arange(SB_H) // (SB_H // SB_G) == 0)[None, :, None]
    sq_pad = jnp.where(group0, jnp.concatenate([sqh, zq], -1), jnp.concatenate([zq, sqh], -1))
    sq_pad = sq_pad.reshape(d, SB_H * HEAD_PAD)
    kr_rot = jnp.concatenate([-kr[:, half:], kr[:, :half]], axis=1)
    tail = HEAD_PAD - MLA_DN - MLA_DR
    kra = jnp.concatenate([z(d, MLA_DN), kr, z(d, tail)], axis=1)
    krb = jnp.concatenate([z(d, MLA_DN), kr_rot, z(d, tail)], axis=1)
    w_h = jnp.concatenate([gates, sq_pad, cb, cc, cx, ckv, sk, sv, kra, krb, z(d, C_CQ - C_KRB - LANES), cq],
                          axis=1).astype(BF16)

    qn_, qr_ = w_uq[:, :, :MLA_DN], w_uq[:, :, MLA_DN:]
    qr_rot = jnp.concatenate([-qr_[:, :, half:], qr_[:, :, :half]], axis=-1)
    zn = jnp.zeros((MLA_QR, MLA_H, MLA_DN), F32)
    zt = jnp.zeros((MLA_QR, MLA_H, tail), F32)
    wuq_a = jnp.concatenate([qn_, qr_, zt], axis=-1).reshape(MLA_QR, MLA_H * HEAD_PAD)
    wuq_b = jnp.concatenate([zn, qr_rot, zt], axis=-1).reshape(MLA_QR, MLA_H * HEAD_PAD)
    wuq = jnp.concatenate([wuq_a, wuq_b], axis=1).astype(BF16)
    zk = jnp.zeros((MLA_KVR, MLA_H, HEAD_PAD - MLA_DN), F32)
    wuk = jnp.concatenate([w_uk, zk], axis=-1).reshape(MLA_KVR, MLA_H * HEAD_PAD).astype(BF16)
    wuv = jnp.concatenate([w_uv, zk], axis=-1).reshape(MLA_KVR, MLA_H * HEAD_PAD).astype(BF16)

    eye = jnp.eye(MLA_DR, dtype=F32)
    top = jnp.concatenate([jnp.transpose(w_uk, (1, 2, 0)), jnp.zeros((MLA_H, MLA_DN, LANES), F32)], axis=-1)
    mid = jnp.concatenate([jnp.zeros((MLA_H, MLA_DR, MLA_KVR), F32),
                           jnp.broadcast_to(eye, (MLA_H, MLA_DR, MLA_DR)),
                           jnp.zeros((MLA_H, MLA_DR, LANES - MLA_DR), F32)], axis=-1)
    bot = jnp.zeros((MLA_H, tail, MLA_KVR + LANES), F32)
    w_qext = jnp.concatenate([top, mid, bot], axis=1).astype(BF16)
    w_uvd = jnp.concatenate([jnp.transpose(w_uv, (1, 0, 2)),
                             jnp.zeros((MLA_H, MLA_KVR, HEAD_PAD - MLA_DV), F32)], axis=-1).astype(BF16)

    wbm = jnp.concatenate([w_bm.reshape(MLA_H, MLA_DV, D_MODEL),
                           jnp.zeros((MLA_H, HEAD_PAD - MLA_DV, D_MODEL), F32)], axis=1)
    wbm = wbm.reshape(MLA_H * HEAD_PAD, D_MODEL).astype(BF16)
    wsh = w_bs.reshape(SB_H, SB_D, D_MODEL)
    zs = jnp.zeros_like(wsh)
    wbs = jnp.where(group0.reshape(SB_H, 1, 1), jnp.concatenate([wsh, zs], 1), jnp.concatenate([zs, wsh], 1))
    wbs = wbs.reshape(SB_H * HEAD_PAD, D_MODEL).astype(BF16)
    rwp = jnp.concatenate([rw, z(d, LANES - N_EXPERTS)], axis=1)
    rwh = rwp.astype(BF16)
    rwl = (rwp - rwh.astype(F32)).astype(BF16)
    rbp = jnp.concatenate([rb, jnp.full((LANES - N_EXPERTS,), NEG_BIG, F32)]).reshape(1, LANES)
    cw8 = jnp.concatenate([conv_w, jnp.zeros((8 - conv_w.shape[0], CONV_DIM), F32)], axis=0)
    return dict(w_h=w_h, qn=q_norm.reshape(1, -1), kvn=kv_norm.reshape(1, -1), wuq=wuq, wuk=wuk, wuv=wuv, cw=cw8,
                w_qext=w_qext, w_uvd=w_uvd, wbm=wbm, wbc=w_bc.astype(BF16), wbs=wbs, wo=w_o.astype(BF16),
                rwh=rwh, rwl=rwl, rb=rbp)


def _rope_tables(pos):
    half = MLA_DR // 2
    freq = ROPE_THETA ** (-jnp.arange(half, dtype=F32) / half)
    ang = pos.astype(F32)[:, None] * freq
    n = pos.shape[0]
    ones = jnp.ones((n, MLA_DN), F32)
    tail = HEAD_PAD - MLA_DN - MLA_DR
    cos_t = jnp.concatenate([ones, jnp.cos(ang), jnp.cos(ang), jnp.ones((n, tail), F32)], axis=1)
    sin_t = jnp.concatenate([jnp.zeros((n, MLA_DN), F32), jnp.sin(ang), jnp.sin(ang), jnp.zeros((n, tail), F32)],
                            axis=1)
    return cos_t, sin_t


def _group_tail(xg, h, om, oc, osb, lw, lay, experts, ln1, ln2, *, tm, chunk, alpha):
    x1, x1p, idx8, gw8 = _merge(om, oc, osb, h, xg, lw["wbm"], lw["wbc"], lw["wbs"], lw["wo"], ln1[0], ln1[1],
                                lw["rwh"], lw["rwl"], lw["rb"], tm=tm, alpha=alpha)
    t = xg.shape[0]
    n_chunks = t // chunk
    idx = idx8[:, :TOP_K].reshape(n_chunks, chunk, TOP_K)
    gw = gw8[:, :TOP_K].reshape(n_chunks, chunk, TOP_K)
    tile_e, n_tiles, src, nval, tok, gws = jax.vmap(_route)(idx, gw)
    pack_rows = D_MODEL // 2 // LANES
    x1p = x1p.reshape(n_chunks, chunk * pack_rows, LANES)
    wg, bg, wu, bu, wd, bd = experts
    outs = [_moe(tile_e[c], n_tiles[c], src[c], nval[c], tok[c], lay, gws[c], x1p[c], wg, bg, wu, bu, wd, bd)
            for c in range(n_chunks)]
    keep = chunk * (D_MODEL // LANES)
    ffn2 = outs[0] if n_chunks == 1 else jnp.concatenate([o[:keep] for o in outs], axis=0)
    return _ln2(ffn2, x1, ln2[0], ln2[1], tm=tm, alpha=alpha)


def kernel(x_prompt, x_sample, cache_mla_ckv, cache_mla_krope, cache_sb_k, cache_sb_v, state_conv, page_table, w_in, mla_q_norm, mla_kv_norm, mla_w_uq, mla_w_uk, mla_w_uv, conv_w, w_branch_mla, w_branch_conv, w_branch_sb, w_o, ln1_g, ln1_b, router_w, router_b, exp_w_gate, exp_b_gate, exp_w_up, exp_b_up, exp_w_down, exp_b_down, ln2_g, ln2_b):
    depth = w_in.shape[0]
    batch, seq, d = x_prompt.shape
    nb = x_sample.shape[0]
    n_pages = page_table.shape[1]
    past = n_pages * PAGE
    alpha = (2 * depth) ** 0.25
    tp = batch * seq

    n_pool = cache_sb_k.shape[1]
    c_krt = jnp.transpose(cache_mla_krope, (0, 1, 3, 2))
    c_kt = jnp.transpose(cache_sb_k, (0, 1, 3, 4, 2)).reshape(depth, n_pool, SB_G * SB_D, PAGE)
    c_vt = jnp.transpose(cache_sb_v, (0, 1, 3, 4, 2)).reshape(depth, n_pool, SB_G * SB_D, PAGE)
    pt_flat = page_table.reshape(-1).astype(jnp.int32)

    experts = (exp_w_gate.astype(BF16), exp_b_gate.reshape(depth, N_EXPERTS, 1, d),
               exp_w_up.astype(BF16), exp_b_up.reshape(depth, N_EXPERTS, 1, d),
               exp_w_down.astype(BF16), exp_b_down.reshape(depth, N_EXPERTS, 1, d))

    cos_p, sin_p = _rope_tables(jnp.arange(seq))
    cos_s, sin_s = _rope_tables(jnp.full((nb,), past))

    xp = x_prompt.reshape(tp, d)
    xs = x_sample.reshape(nb, d)
    new_p = [[], [], [], [], []]
    new_s = [[], [], [], [], []]
    for l in range(depth):
        lay = jnp.full((1,), l, jnp.int32)
        lw = _prep_layer_weights(w_in[l], mla_q_norm[l], mla_kv_norm[l], mla_w_uq[l], mla_w_uk[l], mla_w_uv[l],
                                 conv_w[l], w_branch_mla[l], w_branch_conv[l], w_branch_sb[l], w_o[l],
                                 router_w[l], router_b[l])
        ln1 = (ln1_g[l].reshape(1, d), ln1_b[l].reshape(1, d))
        ln2 = (ln2_g[l].reshape(1, d), ln2_b[l].reshape(1, d))

        h = _linear(xp, lw["w_h"], F32, 1024, 768)
        ckv, kr, q, k, v, oc, tail, sq, sk, sv = _prep(
            h, cos_p, sin_p, lw["qn"], lw["kvn"], lw["wuq"], lw["wuk"], lw["wuv"], lw["cw"], None,
            tm=512, decode=False, tiles_per_seq=seq // 512)
        om = _mla_attn(q, k, v, batch=batch, seq=seq, tq=512)
        osb = _sb_attn(sq, sk, sv, batch=batch, seq=seq, tq=256)
        new_p[0].append(ckv.reshape(batch, seq, MLA_KVR))
        new_p[1].append(kr.reshape(batch, seq, MLA_DR))
        new_p[2].append(h[:, C_SBK:C_SBK + LANES].reshape(batch, seq, SB_G, SB_D))
        new_p[3].append(h[:, C_SBV:C_SBV + LANES].reshape(batch, seq, SB_G, SB_D))
        new_p[4].append(tail.reshape(batch, seq // 512, 8, CONV_DIM)[:, -1, 6:, :])
        xp = _group_tail(xp, h, om, oc, osb, lw, lay, experts, ln1, ln2, tm=512, chunk=seq, alpha=alpha)

        hs = _linear(xs, lw["w_h"], F32, nb, 768)
        st = state_conv[l]
        ckv_s, kr_s, q_s, _, _, oc_s, u_s, sq_s, _, _ = _prep(
            hs, cos_s, sin_s, lw["qn"], lw["kvn"], lw["wuq"], lw["wuk"], lw["wuv"], lw["cw"],
            (st[:, 0, :], st[:, 1, :]), tm=nb, decode=True, tiles_per_seq=1)
        qe = _headwise_linear(q_s, lw["w_qext"], BF16).reshape(nb, MLA_H, MLA_KVR + LANES)
        olat, osb_s = _decode_attn(pt_flat, lay, qe, sq_s.reshape(nb, SB_H, HEAD_PAD),
                                   ckv_s.reshape(nb, 1, MLA_KVR), kr_s.reshape(nb, 1, MLA_DR),
                                   cache_mla_ckv, c_krt, c_kt, c_vt, pages_per_step=min(32, n_pages))
        om_s = _headwise_linear(olat.reshape(nb, MLA_H * MLA_KVR), lw["w_uvd"], BF16)
        new_s[0].append(ckv_s.reshape(nb, 1, MLA_KVR))
        new_s[1].append(kr_s.reshape(nb, 1, MLA_DR))
        new_s[2].append(hs[:, C_SBK:C_SBK + LANES].reshape(nb, 1, SB_G, SB_D))
        new_s[3].append(hs[:, C_SBV:C_SBV + LANES].reshape(nb, 1, SB_G, SB_D))
        new_s[4].append(jnp.stack([st[:, 1, :], u_s], axis=1))
        xs = _group_tail(xs, hs, om_s, oc_s, osb_s.reshape(nb, SB_H * HEAD_PAD), lw, lay, experts, ln1, ln2,
                         tm=nb, chunk=nb, alpha=alpha)

    outs_p = [jnp.stack(a) for a in new_p]
    outs_s = [jnp.stack(a) for a in new_s]
    return (xp.reshape(batch, seq, d), xs.reshape(nb, 1, d), *outs_p, *outs_s)
```
